```python
import math
import jax, jax.numpy as jnp
from jax import lax
import numpy as np

D_MODEL = 1024
BATCH = 8
SEQ = 2048
DEPTH = 1
DEC_BATCH = 128
DEC_SEQ = 1
PAST_LEN = 16384
PAGE_SIZE = 128

CONV_DIM = D_MODEL // 2
CONV_GROUPS = 8
CONV_WIDTH = 3
N_HEADS = 8
V_HEAD_DIM = (D_MODEL - CONV_DIM) // N_HEADS
QK_NOPE_DIM = 64
QK_ROPE_DIM = 32
Q_RANK = 384
KV_RANK = 256
MIX_DIM = CONV_DIM + N_HEADS * V_HEAD_DIM
IN_DIM = 3 * CONV_DIM + Q_RANK + KV_RANK + QK_ROPE_DIM
D_FF = -(-8 * D_MODEL // 768) * 256
ROPE_THETA = 10000.0
Q_BLOCK = 128
LN_EPS = 1e-5
RMS_EPS = 1e-6
ALPHA = (2 * DEPTH) ** 0.25
BETA = (8 * DEPTH) ** -0.25
SM_SCALE = (QK_NOPE_DIM + QK_ROPE_DIM) ** -0.5
NEG_INF = -1e30

kernel_name = 'hymba_conv_mla_deepnorm_adaln_step'


def layer_norm(x, g, b):
    xf = x.astype(jnp.float32)
    mu = jnp.mean(xf, axis=-1, keepdims=True)
    var = jnp.mean(jnp.square(xf - mu), axis=-1, keepdims=True)
    return ((xf - mu) * lax.rsqrt(var + LN_EPS)).astype(x.dtype) * g + b


def rms_norm(x, g):
    xf = x.astype(jnp.float32)
    return (xf * lax.rsqrt(jnp.mean(jnp.square(xf), axis=-1, keepdims=True) + RMS_EPS)).astype(x.dtype) * g


def rope_tables(pos, dtype):
    inv = 1.0 / (ROPE_THETA ** (jnp.arange(0, QK_ROPE_DIM, 2, dtype=jnp.float32) / QK_ROPE_DIM))
    ang = pos.astype(jnp.float32)[:, None] * inv[None, :]
    return jnp.cos(ang).astype(dtype), jnp.sin(ang).astype(dtype)


def apply_rope(x, cos, sin):
    x1, x2 = jnp.split(x, 2, axis=-1)
    return jnp.concatenate([x1 * cos - x2 * sin, x2 * cos + x1 * sin], axis=-1)


def project(u, w_in, g_q, g_kv, w_uq, cos, sin):
    splits = [CONV_DIM, 2 * CONV_DIM, 3 * CONV_DIM, 3 * CONV_DIM + Q_RANK, 3 * CONV_DIM + Q_RANK + KV_RANK]
    h, gb, gc, cq, ckv, kr = jnp.split(u @ w_in, splits, axis=-1)
    conv_in = gc * h
    q = jnp.einsum('bsr,rhd->bshd', rms_norm(cq, g_q), w_uq)
    q_nope = q[..., :QK_NOPE_DIM]
    q_rope = apply_rope(q[..., QK_NOPE_DIM:], cos[:, None, :], sin[:, None, :])
    latent = rms_norm(ckv, g_kv)
    k_rope = apply_rope(kr, cos, sin)
    return gb, conv_in, q_nope, q_rope, latent, k_rope


def short_conv(prefix, conv_in, gb, conv_w):
    full = jnp.concatenate([prefix, conv_in], axis=1)
    s = conv_in.shape[1]
    y = conv_w[0] * full[:, 0:s]
    for k in range(1, CONV_WIDTH):
        y = y + conv_w[k] * full[:, k:k + s]
    return gb * y, full[:, -(CONV_WIDTH - 1):]


def mla_prompt(q_nope, q_rope, latent, k_rope, w_uk, w_uv):
    b, s = q_nope.shape[0], q_nope.shape[1]
    k_nope = jnp.einsum('bsr,rhd->bshd', latent, w_uk)
    v = jnp.einsum('bsr,rhd->bshd', latent, w_uv)
    nblk = s // Q_BLOCK

    def to_blocks(t):
        return jnp.moveaxis(t.reshape((b, nblk, Q_BLOCK) + t.shape[2:]), 1, 0)

    kpos = jnp.arange(s)

    def one_block(args):
        qn, qr, start = args
        sc = (jnp.einsum('bqhd,bkhd->bhqk', qn, k_nope)
              + jnp.einsum('bqhd,bkd->bhqk', qr, k_rope)).astype(jnp.float32) * SM_SCALE
        qpos = start + jnp.arange(Q_BLOCK)
        sc = jnp.where(kpos[None, :] <= qpos[:, None], sc, NEG_INF)
        p = jax.nn.softmax(sc, axis=-1).astype(v.dtype)
        return jnp.einsum('bhqk,bkhd->bqhd', p, v)

    o = lax.map(one_block, (to_blocks(q_nope), to_blocks(q_rope), jnp.arange(nblk) * Q_BLOCK))
    return jnp.moveaxis(o, 0, 1).reshape(b, s, N_HEADS * V_HEAD_DIM)


def mla_sample(q_nope, q_rope, latent, k_rope, lat_past, kr_past, w_uk, w_uv):
    b, t = q_nope.shape[0], q_nope.shape[1]
    q_lat = jnp.einsum('bqhd,rhd->bqhr', q_nope, w_uk)
    s_past = (jnp.einsum('bqhr,bkr->bhqk', q_lat, lat_past)
              + jnp.einsum('bqhd,bkd->bhqk', q_rope, kr_past)).astype(jnp.float32) * SM_SCALE
    s_new = (jnp.einsum('bqhr,bkr->bhqk', q_lat, latent)
             + jnp.einsum('bqhd,bkd->bhqk', q_rope, k_rope)).astype(jnp.float32) * SM_SCALE
    causal = jnp.arange(t)[None, :] <= jnp.arange(t)[:, None]
    s_new = jnp.where(causal, s_new, NEG_INF)
    p = jax.nn.softmax(jnp.concatenate([s_past, s_new], axis=-1), axis=-1).astype(latent.dtype)
    n_past = lat_past.shape[1]
    o_lat = (jnp.einsum('bhqk,bkr->bqhr', p[..., :n_past], lat_past)
             + jnp.einsum('bhqk,bkr->bqhr', p[..., n_past:], latent))
    return jnp.einsum('bqhr,rhd->bqhd', o_lat, w_uv).reshape(b, t, N_HEADS * V_HEAD_DIM)


def run_layer(x, c, pos, conv_prefix, attend, lw):
    mod = (jax.nn.silu(c) @ lw['w_ada'] + lw['b_ada'])[:, None, :]
    sh_a, sc_a, g_a, sh_f, sc_f, g_f = jnp.split(mod, 6, axis=-1)
    cos, sin = rope_tables(pos, x.dtype)
    u = x * (1 + sc_a) + sh_a
    gb, conv_in, qn, qr, lat, kr = project(u, lw['w_in'], lw['g_q'], lw['g_kv'], lw['w_uq'], cos, sin)
    conv_out, conv_state = short_conv(conv_prefix, conv_in, gb, lw['conv_w'])
    attn_out = attend(qn, qr, lat, kr)
    a = jnp.concatenate([conv_out, attn_out], axis=-1) @ lw['w_o']
    x = layer_norm(ALPHA * x + (1 + g_a) * a, lw['ln1_g'], lw['ln1_b'])
    u = x * (1 + sc_f) + sh_f
    f = (jax.nn.silu(u @ lw['w_gate']) * (u @ lw['w_up'])) @ lw['w_down']
    x = layer_norm(ALPHA * x + (1 + g_f) * f, lw['ln2_g'], lw['ln2_b'])
    return x, lat, kr, conv_state


def setup_inputs(seed: int = 0) -> dict:
    key = jax.random.key(seed)
    ks = jax.random.split(key, 26)
    f32 = jnp.float32
    n_pages = PAST_LEN // PAGE_SIZE
    n_phys = (5 * DEC_BATCH * n_pages) // 4

    def nrm(k, shape, s):
        return jax.random.normal(k, shape, f32) * s

    page_table = jax.random.permutation(ks[5], n_phys)[:DEC_BATCH * n_pages].reshape(DEC_BATCH, n_pages).astype(jnp.int32)
    return {
        'x_prompt': nrm(ks[0], (BATCH, SEQ, D_MODEL), 1.0),
        'x_sample': nrm(ks[1], (DEC_BATCH, DEC_SEQ, D_MODEL), 1.0),
        'cache_latent': nrm(ks[2], (DEPTH, n_phys, PAGE_SIZE, KV_RANK), 1.0),
        'cache_k_rope': nrm(ks[3], (DEPTH, n_phys, PAGE_SIZE, QK_ROPE_DIM), 1.0),
        'state_conv': nrm(ks[4], (DEPTH, DEC_BATCH, CONV_WIDTH - 1, CONV_DIM), 1.0),
        'page_table': page_table,
        'c_prompt': nrm(ks[6], (BATCH, D_MODEL), 1.0),
        'c_sample': nrm(ks[7], (DEC_BATCH, D_MODEL), 1.0),
        'w_ada': nrm(ks[8], (DEPTH, D_MODEL, 6 * D_MODEL), 0.1 * D_MODEL ** -0.5),
        'b_ada': nrm(ks[9], (DEPTH, 6 * D_MODEL), 0.01),
        'w_in': nrm(ks[10], (DEPTH, D_MODEL, IN_DIM), D_MODEL ** -0.5),
        'conv_w': nrm(ks[11], (DEPTH, CONV_WIDTH, CONV_DIM), CONV_WIDTH ** -0.5),
        'g_q': 1.0 + nrm(ks[12], (DEPTH, Q_RANK), 0.02),
        'g_kv': 1.0 + nrm(ks[13], (DEPTH, KV_RANK), 0.02),
        'w_uq': nrm(ks[14], (DEPTH, Q_RANK, N_HEADS, QK_NOPE_DIM + QK_ROPE_DIM), Q_RANK ** -0.5),
        'w_uk': nrm(ks[15], (DEPTH, KV_RANK, N_HEADS, QK_NOPE_DIM), KV_RANK ** -0.5),
        'w_uv': nrm(ks[16], (DEPTH, KV_RANK, N_HEADS, V_HEAD_DIM), KV_RANK ** -0.5),
        'w_o': nrm(ks[17], (DEPTH, MIX_DIM, D_MODEL), BETA * MIX_DIM ** -0.5),
        'ln1_g': 1.0 + nrm(ks[18], (DEPTH, D_MODEL), 0.02),
        'ln1_b': nrm(ks[19], (DEPTH, D_MODEL), 0.01),
        'w_gate': nrm(ks[20], (DEPTH, D_MODEL, D_FF), D_MODEL ** -0.5),
        'w_up': nrm(ks[21], (DEPTH, D_MODEL, D_FF), D_MODEL ** -0.5),
        'w_down': nrm(ks[22], (DEPTH, D_FF, D_MODEL), BETA * D_FF ** -0.5),
        'ln2_g': 1.0 + nrm(ks[23], (DEPTH, D_MODEL), 0.02),
        'ln2_b': nrm(ks[24], (DEPTH, D_MODEL), 0.01),
    }


def reference(x_prompt, x_sample, cache_latent, cache_k_rope, state_conv, page_table,
              c_prompt, c_sample, w_ada, b_ada, w_in, conv_w, g_q, g_kv, w_uq, w_uk, w_uv,
              w_o, ln1_g, ln1_b, w_gate, w_up, w_down, ln2_g, ln2_b):
    pos_p = jnp.arange(x_prompt.shape[1])
    pos_s = PAST_LEN + jnp.arange(x_sample.shape[1])
    xp, xs = x_prompt, x_sample
    lat_p_l, kr_p_l, conv_p_l, lat_s_l, kr_s_l, conv_s_l = [], [], [], [], [], []
    for l in range(DEPTH):
        lw = {'w_ada': w_ada[l], 'b_ada': b_ada[l], 'w_in': w_in[l], 'conv_w': conv_w[l],
              'g_q': g_q[l], 'g_kv': g_kv[l], 'w_uq': w_uq[l], 'w_o': w_o[l],
              'ln1_g': ln1_g[l], 'ln1_b': ln1_b[l], 'w_gate': w_gate[l], 'w_up': w_up[l],
              'w_down': w_down[l], 'ln2_g': ln2_g[l], 'ln2_b': ln2_b[l]}
        uk, uv = w_uk[l], w_uv[l]
        prefix_p = jnp.zeros((xp.shape[0], CONV_WIDTH - 1, CONV_DIM), xp.dtype)
        attend_p = lambda qn, qr, lat, kr: mla_prompt(qn, qr, lat, kr, uk, uv)
        xp, lat_p, kr_p, conv_p = run_layer(xp, c_prompt, pos_p, prefix_p, attend_p, lw)
        n_seq = page_table.shape[0]
        lat_past = cache_latent[l, page_table].reshape(n_seq, -1, KV_RANK)
        kr_past = cache_k_rope[l, page_table].reshape(n_seq, -1, QK_ROPE_DIM)
        attend_s = lambda qn, qr, lat, kr: mla_sample(qn, qr, lat, kr, lat_past, kr_past, uk, uv)
        xs, lat_s, kr_s, conv_s = run_layer(xs, c_sample, pos_s, state_conv[l], attend_s, lw)
        lat_p_l.append(lat_p); kr_p_l.append(kr_p); conv_p_l.append(conv_p)
        lat_s_l.append(lat_s); kr_s_l.append(kr_s); conv_s_l.append(conv_s)
    return (xp, xs, jnp.stack(lat_p_l), jnp.stack(kr_p_l), jnp.stack(conv_p_l),
            jnp.stack(lat_s_l), jnp.stack(kr_s_l), jnp.stack(conv_s_l))
```

```python
import functools
import math

import jax
import jax.numpy as jnp
from jax import lax
from jax.experimental import pallas as pl
from jax.experimental.pallas import tpu as pltpu

F32 = jnp.float32
BF16 = jnp.bfloat16

ROPE_THETA = 10000.0
LN_EPS = 1e-5
RMS_EPS = 1e-6
NEG_INF = -1e30
LOG2E = math.log2(math.e)

LANES = 128
SUBLANES = 8
HEAD_PAD = 128
VMEM_LIMIT = 56 * 1024 * 1024


def _dot(a, b):
    return jnp.dot(a, b, preferred_element_type=F32)


def _dot_nt(a, b):
    return lax.dot_general(a, b, (((1,), (1,)), ((), ())), preferred_element_type=F32)


def _layer_norm(r, g, b):
    mu = jnp.mean(r, axis=-1, keepdims=True)
    d = r - mu
    var = jnp.mean(d * d, axis=-1, keepdims=True)
    return d * lax.rsqrt(var + LN_EPS) * g + b


def _rms_norm(x, g):
    return x * lax.rsqrt(jnp.mean(x * x, axis=-1, keepdims=True) + RMS_EPS) * g


def _rope_group(x, ta, tb, tc, half):
    return x * ta + pltpu.roll(x, half, 1) * tb + pltpu.roll(x, LANES - half, 1) * tc


def _ada_body(c_ref, w_ref, b_ref, o_ref):
    c = c_ref[...]
    s = c * jax.nn.sigmoid(c)
    o_ref[...] = _dot(s.astype(BF16), w_ref[...].astype(BF16)) + b_ref[...]


def _ada(c_all, w_ada, b_ada, tn=1024):
    m, d = c_all.shape
    n = w_ada.shape[1]
    return pl.pallas_call(
        _ada_body,
        grid=(n // tn,),
        in_specs=[pl.BlockSpec((m, d), lambda j: (0, 0)),
                  pl.BlockSpec((d, tn), lambda j: (0, j)),
                  pl.BlockSpec((1, tn), lambda j: (0, j))],
        out_specs=pl.BlockSpec((m, tn), lambda j: (0, j)),
        out_shape=jax.ShapeDtypeStruct((m, n), F32),
        compiler_params=pltpu.CompilerParams(dimension_semantics=("arbitrary",),
                                             vmem_limit_bytes=VMEM_LIMIT),
        name="ada",
    )(c_all, w_ada, b_ada.reshape(1, n))


def _proj_body(dims, sample, *refs):
    conv_dim, q_rank, kv_rank, n_heads, nope, half, scale = dims
    if sample:
        (x_ref, sh_ref, sc_ref, ta_ref, tb_ref, tc_ref, wmain_ref, wkr_ref, convw_ref, gq_ref, gkv_ref,
         wq_ref, s0_ref, s1_ref, wukt_ref,
         convout_ref, q_ref, qlat_ref, lat_ref, krot_ref, cin_ref) = refs
    else:
        (x_ref, sh_ref, sc_ref, ta_ref, tb_ref, tc_ref, wmain_ref, wkr_ref, convw_ref, gq_ref, gkv_ref,
         wq_ref, wk_ref, wv_ref,
         convout_ref, q_ref, k_ref, v_ref, lat_ref, krot_ref, tail_ref, cin_buf) = refs
    tm = x_ref.shape[1]
    c1, c2, c3 = conv_dim, 2 * conv_dim, 3 * conv_dim
    c4 = c3 + q_rank
    c5 = c4 + kv_rank

    x = x_ref[0]
    u = x * (1.0 + sc_ref[0]) + sh_ref[0]
    ub = u.astype(BF16)
    h = _dot(ub, wmain_ref[:, 0:c1])
    gb = _dot(ub, wmain_ref[:, c1:c2])
    gc = _dot(ub, wmain_ref[:, c2:c3])
    cq = _dot(ub, wmain_ref[:, c3:c4])
    ckv = _dot(ub, wmain_ref[:, c4:c5])
    kr = _dot(ub, wkr_ref[...])

    conv_in = gc * h
    w0 = convw_ref[0:1, :]
    w1 = convw_ref[1:2, :]
    w2 = convw_ref[2:3, :]
    if sample:
        y = w0 * s0_ref[...] + w1 * s1_ref[...] + w2 * conv_in
        cin_ref[0] = conv_in
    else:
        @pl.when(pl.program_id(1) == 0)
        def _():
            cin_buf[0:SUBLANES, :] = jnp.zeros((SUBLANES, conv_dim), F32)
        cin_buf[SUBLANES:SUBLANES + tm, :] = conv_in
        y = (w0 * cin_buf[SUBLANES - 2:SUBLANES - 2 + tm, :]
             + w1 * cin_buf[SUBLANES - 1:SUBLANES - 1 + tm, :] + w2 * conv_in)
        tail = conv_in[tm - SUBLANES:tm, :]
        cin_buf[0:SUBLANES, :] = tail
        tail_ref[0] = tail
    convout_ref[0] = (gb * y).astype(BF16)

    ta = ta_ref[...]
    tb = tb_ref[...]
    tc = tc_ref[...]
    cqn = _rms_norm(cq, gq_ref[...]).astype(BF16)
    qf = _dot(cqn, wq_ref[...])
    q_heads = [_rope_group(qf[:, HEAD_PAD * i:HEAD_PAD * (i + 1)], ta, tb, tc, half) * scale
               for i in range(n_heads)]
    q_ref[0] = jnp.concatenate(q_heads, axis=1).astype(q_ref.dtype)

    latent = _rms_norm(ckv, gkv_ref[...])
    lat_ref[0] = latent
    krot = _rope_group(kr, ta, tb, tc, half)
    krot_ref[0] = krot[:, nope:nope + 2 * half]

    if sample:
        for i in range(n_heads):
            qn = q_heads[i][:, 0:nope].astype(BF16)
            qlat_ref[0, :, kv_rank * i:kv_rank * (i + 1)] = _dot(qn, wukt_ref[i])
    else:
        latb = latent.astype(BF16)
        kf = _dot(latb, wk_ref[...])
        k_ref[0] = (kf + jnp.concatenate([krot] * n_heads, axis=1)).astype(BF16)
        v_ref[0] = _dot(latb, wv_ref[...]).astype(BF16)


def _const_spec(shape):
    nd = len(shape)
    return pl.BlockSpec(shape, lambda *_: (0,) * nd, pipeline_mode=pl.Buffered(1))


def _proj(dims, sample, tm, x, sh, sc, tabs, weights, extra):
    conv_dim, q_rank, kv_rank, n_heads, nope, half, _ = dims
    nb, s, d = x.shape
    r = sh.shape[1]
    grid = (nb, s // tm)
    row = lambda b, i: (b, i, 0)
    mod_spec = pl.BlockSpec((1, r, d), (lambda b, i: (b, 0, 0)))
    tab_spec = pl.BlockSpec((tm, LANES), lambda b, i: (i, 0))
    wmain, wkr, convw, gq, gkv, wq = weights
    in_specs = [pl.BlockSpec((1, tm, d), row), mod_spec, mod_spec, tab_spec, tab_spec, tab_spec,
                _const_spec(wmain.shape), _const_spec(wkr.shape), _const_spec(convw.shape),
                _const_spec(gq.shape), _const_spec(gkv.shape), _const_spec(wq.shape)]
    in_specs += [_const_spec(e.shape) for e in extra]
    hq = n_heads * HEAD_PAD
    out_shape = [jax.ShapeDtypeStruct((nb, s, conv_dim), BF16)]
    out_specs = [pl.BlockSpec((1, tm, conv_dim), row)]
    if sample:
        out_shape += [jax.ShapeDtypeStruct((nb, s, hq), F32),
                      jax.ShapeDtypeStruct((nb, s, n_heads * kv_rank), F32)]
        out_specs += [pl.BlockSpec((1, tm, hq), row), pl.BlockSpec((1, tm, n_heads * kv_rank), row)]
    else:
        out_shape += [jax.ShapeDtypeStruct((nb, s, hq), BF16), jax.ShapeDtypeStruct((nb, s, hq), BF16),
                      jax.ShapeDtypeStruct((nb, s, n_heads * (hq // n_heads // 2)), BF16)]
        out_specs += [pl.BlockSpec((1, tm, hq), row), pl.BlockSpec((1, tm, hq), row),
                      pl.BlockSpec((1, tm, n_heads * (hq // n_heads // 2)), row)]
    out_shape += [jax.ShapeDtypeStruct((nb, s, kv_rank), F32), jax.ShapeDtypeStruct((nb, s, 2 * half), F32)]
    out_specs += [pl.BlockSpec((1, tm, kv_rank), row), pl.BlockSpec((1, tm, 2 * half), row)]
    scratch = []
    if sample:
        out_shape += [jax.ShapeDtypeStruct((nb, s, conv_dim), F32)]
        out_specs += [pl.BlockSpec((1, tm, conv_dim), row)]
    else:
        out_shape += [jax.ShapeDtypeStruct((nb, SUBLANES, conv_dim), F32)]
        out_specs += [pl.BlockSpec((1, SUBLANES, conv_dim), lambda b, i: (b, 0, 0))]
        scratch = [pltpu.VMEM((SUBLANES + tm, conv_dim), F32)]
    return pl.pallas_call(
        functools.partial(_proj_body, dims, sample),
        grid=grid, in_specs=in_specs, out_specs=out_specs, out_shape=out_shape,
        scratch_shapes=scratch,
        compiler_params=pltpu.CompilerParams(dimension_semantics=("arbitrary", "arbitrary"),
                                             vmem_limit_bytes=VMEM_LIMIT),
        name="proj_sample" if sample else "proj_prompt",
    )(x, sh, sc, *tabs, *weights, *extra)


def _attn_prompt_body(v_dim, q_ref, k_ref, v_ref, o_ref):
    tq = q_ref.shape[1]
    tk = tq
    qi = pl.program_id(2)
    rows = lax.broadcasted_iota(jnp.int32, (tq, tk), 0)
    cols = lax.broadcasted_iota(jnp.int32, (tq, tk), 1)
    outs = []
    for hh in range(2):
        q = q_ref[0, :, HEAD_PAD * hh:HEAD_PAD * (hh + 1)]

        def step(j, carry, masked):
            m, l, acc = carry
            k = k_ref[0, pl.ds(pl.multiple_of(j * tk, tk), tk), HEAD_PAD * hh:HEAD_PAD * (hh + 1)]
            v = v_ref[0, pl.ds(pl.multiple_of(j * tk, tk), tk), v_dim * hh:v_dim * (hh + 1)]
            s = _dot_nt(q, k)
            if masked:
                s = jnp.where(cols <= rows, s, NEG_INF)
            m_new = jnp.maximum(m, jnp.max(s, axis=-1, keepdims=True))
            p = jnp.exp2(s - m_new)
            alpha = jnp.exp2(m - m_new)
            l = alpha * l + jnp.sum(p, axis=-1, keepdims=True)
            acc = alpha * acc + _dot(p.astype(BF16), v)
            return m_new, l, acc

        init = (jnp.full((tq, 1), NEG_INF, F32), jnp.zeros((tq, 1), F32), jnp.zeros((tq, v_dim), F32))
        carry = lax.fori_loop(0, qi, functools.partial(step, masked=False), init)
        m, l, acc = step(qi, carry, True)
        outs.append(acc / l)
    o_ref[0] = jnp.concatenate(outs, axis=1).astype(o_ref.dtype)


def _attn_prompt(q, k, v, n_heads, tq=512):
    b, s, _ = q.shape
    v_dim = v.shape[2] // n_heads
    return pl.pallas_call(
        functools.partial(_attn_prompt_body, v_dim),
        grid=(b, n_heads // 2, s // tq),
        in_specs=[pl.BlockSpec((1, tq, 2 * HEAD_PAD), lambda bi, hp, qi: (bi, qi, hp)),
                  pl.BlockSpec((1, s, 2 * HEAD_PAD), lambda bi, hp, qi: (bi, 0, hp)),
                  pl.BlockSpec((1, s, 2 * v_dim), lambda bi, hp, qi: (bi, 0, hp))],
        out_specs=pl.BlockSpec((1, tq, 2 * v_dim), lambda bi, hp, qi: (bi, qi, hp)),
        out_shape=jax.ShapeDtypeStruct((b, s, n_heads * v_dim), BF16),
        compiler_params=pltpu.CompilerParams(dimension_semantics=("arbitrary",) * 3,
                                             vmem_limit_bytes=VMEM_LIMIT),
        name="attn_prompt",
    )(q, k, v)


def _attn_sample_body(geom, pt_ref, q_ref, qlat_ref, latn_ref, krn_ref, lat_hbm, kr_hbm, o_ref,
                      lat_buf, kr_buf, sems):
    n_pages, ch, page, nope, rope = geom
    n_chunks = n_pages // ch
    b = pl.program_id(0)
    nb = pl.num_programs(0)
    n_heads = o_ref.shape[1]

    def page_copies(seq, c, slot, j):
        pg = pt_ref[seq, c * ch + j]
        return (pltpu.make_async_copy(lat_hbm.at[pg], lat_buf.at[slot, j], sems.at[0, slot]),
                pltpu.make_async_copy(kr_hbm.at[pg], kr_buf.at[slot, j], sems.at[1, slot]))

    def start_chunk(seq, c, slot):
        for j in range(ch):
            for cp in page_copies(seq, c, slot, j):
                cp.start()

    def wait_chunk(slot):
        for j in range(ch):
            for cp in page_copies(0, 0, slot, j):
                cp.wait()

    @pl.when(b == 0)
    def _():
        start_chunk(0, 0, 0)

    q = q_ref[0]
    qr = q[:, nope:nope + rope]
    qlat = qlat_ref[0]
    latn = latn_ref[0]
    krn = krn_ref[0]
    s_new = (jnp.sum(qlat * latn, axis=-1, keepdims=True)
             + jnp.sum(qr * krn, axis=-1, keepdims=True))
    qlat_b = qlat.astype(BF16)
    qr_b = qr.astype(BF16)
    init = (s_new, jnp.ones_like(s_new), jnp.broadcast_to(latn, qlat.shape))

    def chunk(c, carry):
        m, l, acc = carry
        slot = c % 2

        @pl.when(c + 1 < n_chunks)
        def _():
            start_chunk(b, c + 1, 1 - slot)

        @pl.when(jnp.logical_and(c + 1 == n_chunks, b + 1 < nb))
        def _():
            start_chunk(b + 1, 0, 1 - slot)

        wait_chunk(slot)
        latb = lat_buf[slot].reshape(ch * page, lat_buf.shape[-1]).astype(BF16)
        krb = kr_buf[slot].reshape(ch * page, rope).astype(BF16)
        s = _dot_nt(qlat_b, latb) + _dot_nt(qr_b, krb)
        m_new = jnp.maximum(m, jnp.max(s, axis=-1, keepdims=True))
        p = jnp.exp2(s - m_new)
        alpha = jnp.exp2(m - m_new)
        l = alpha * l + jnp.sum(p, axis=-1, keepdims=True)
        acc = alpha * acc + _dot(p.astype(BF16), latb)
        return m_new, l, acc

    m, l, acc = lax.fori_loop(0, n_chunks, chunk, init)
    o_ref[0] = (acc / l)[0:n_heads, :]


def _attn_sample(page_table, q16, qlat16, lat_new, kr_new, cache_lat, cache_kr, n_heads, nope, ch=16):
    nseq, n_pages = page_table.shape
    _, page, kv_rank = cache_lat.shape
    rope = cache_kr.shape[-1]
    assert n_pages % ch == 0 and (n_pages // ch) % 2 == 0
    hp = q16.shape[1]
    geom = (n_pages, ch, page, nope, rope)
    grid_spec = pltpu.PrefetchScalarGridSpec(
        num_scalar_prefetch=1,
        grid=(nseq,),
        in_specs=[pl.BlockSpec((1, hp, HEAD_PAD), lambda b, pt: (b, 0, 0)),
                  pl.BlockSpec((1, hp, kv_rank), lambda b, pt: (b, 0, 0)),
                  pl.BlockSpec((1, 1, kv_rank), lambda b, pt: (b, 0, 0)),
                  pl.BlockSpec((1, 1, rope), lambda b, pt: (b, 0, 0)),
                  pl.BlockSpec(memory_space=pl.ANY),
                  pl.BlockSpec(memory_space=pl.ANY)],
        out_specs=pl.BlockSpec((1, n_heads, kv_rank), lambda b, pt: (b, 0, 0)),
        scratch_shapes=[pltpu.VMEM((2, ch, page, kv_rank), F32),
                        pltpu.VMEM((2, ch, page, rope), F32),
                        pltpu.SemaphoreType.DMA((2, 2))],
    )
    return pl.pallas_call(
        functools.partial(_attn_sample_body, geom),
        grid_spec=grid_spec,
        out_shape=jax.ShapeDtypeStruct((nseq, n_heads, kv_rank), F32),
        compiler_params=pltpu.CompilerParams(dimension_semantics=("arbitrary",),
                                             vmem_limit_bytes=VMEM_LIMIT),
        name="attn_sample",
    )(page_table, q16, qlat16, lat_new, kr_new, cache_lat, cache_kr)


def _post_body(alpha, ff_chunks, sample, *refs):
    if sample:
        (x_ref, convout_ref, olat_ref, ga_ref, shf_ref, scf_ref, gf_ref, woc_ref, woa_ref,
         ln1g_ref, ln1b_ref, wg_ref, wu_ref, wd_ref, ln2g_ref, ln2b_ref, wuv_ref, y_ref) = refs
        attn = _dot(olat_ref[0].astype(BF16), wuv_ref[...]).astype(BF16)
    else:
        (x_ref, convout_ref, attn_ref, ga_ref, shf_ref, scf_ref, gf_ref, woc_ref, woa_ref,
         ln1g_ref, ln1b_ref, wg_ref, wu_ref, wd_ref, ln2g_ref, ln2b_ref, y_ref) = refs
        attn = attn_ref[0]
    x = x_ref[0]
    a = _dot(convout_ref[0], woc_ref[...]) + _dot(attn, woa_ref[...])
    x1 = _layer_norm(alpha * x + (1.0 + ga_ref[0]) * a, ln1g_ref[...], ln1b_ref[...])
    ub = (x1 * (1.0 + scf_ref[0]) + shf_ref[0]).astype(BF16)
    f = None
    for lo, hi in ff_chunks:
        g = _dot(ub, wg_ref[:, lo:hi])
        up = _dot(ub, wu_ref[:, lo:hi])
        hmid = (g * jax.nn.sigmoid(g) * up).astype(BF16)
        part = _dot(hmid, wd_ref[lo:hi, :])
        f = part if f is None else f + part
    y_ref[0] = _layer_norm(alpha * x1 + (1.0 + gf_ref[0]) * f, ln2g_ref[...], ln2b_ref[...])


def _post(alpha, sample, tm, x, convout, attn, mods, weights, extra):
    nb, s, d = x.shape
    r = mods[0].shape[1]
    d_ff = weights[5].shape[1]
    n_ff = 2 if d_ff % (2 * LANES) == 0 else 1
    ff_chunks = tuple((i * d_ff // n_ff, (i + 1) * d_ff // n_ff) for i in range(n_ff))
    row = lambda b, i: (b, i, 0)
    mod_spec = pl.BlockSpec((1, r, d), lambda b, i: (b, 0, 0))
    in_specs = [pl.BlockSpec((1, tm, d), row),
                pl.BlockSpec((1, tm, convout.shape[2]), row),
                pl.BlockSpec((1, tm, attn.shape[2]), row),
                mod_spec, mod_spec, mod_spec, mod_spec]
    in_specs += [_const_spec(w.shape) for w in weights]
    in_specs += [_const_spec(e.shape) for e in extra]
    return pl.pallas_call(
        functools.partial(_post_body, alpha, ff_chunks, sample),
        grid=(nb, s // tm), in_specs=in_specs,
        out_specs=pl.BlockSpec((1, tm, d), row),
        out_shape=jax.ShapeDtypeStruct((nb, s, d), F32),
        compiler_params=pltpu.CompilerParams(dimension_semantics=("arbitrary", "arbitrary"),
                                             vmem_limit_bytes=VMEM_LIMIT),
        name="post_sample" if sample else "post_prompt",
    )(x, convout, attn, *mods, *weights, *extra)


def _rope_tables(pos, rope, nope):
    half = rope // 2
    inv = 1.0 / (ROPE_THETA ** (jnp.arange(0, rope, 2, dtype=F32) / rope))
    ang = pos.astype(F32)[:, None] * inv[None, :]
    cos, sin = jnp.cos(ang), jnp.sin(ang)
    n = pos.shape[0]
    z = lambda w: jnp.zeros((n, w), F32)
    pad = LANES - nope - rope
    ta = jnp.concatenate([jnp.ones((n, nope), F32), cos, cos, z(pad)], axis=1)
    tb = jnp.concatenate([z(nope + half), sin, z(pad)], axis=1)
    tc = jnp.concatenate([z(nope), -sin, z(half + pad)], axis=1)
    return ta, tb, tc


def kernel(x_prompt, x_sample, cache_latent, cache_k_rope, state_conv, page_table, c_prompt, c_sample,
           w_ada, b_ada, w_in, conv_w, g_q, g_kv, w_uq, w_uk, w_uv, w_o, ln1_g, ln1_b, w_gate, w_up,
           w_down, ln2_g, ln2_b):
    depth = w_ada.shape[0]
    nb, seq, d = x_prompt.shape
    ns, dec_seq, _ = x_sample.shape
    assert dec_seq == 1
    conv_dim = conv_w.shape[2]
    q_rank = g_q.shape[1]
    kv_rank = g_kv.shape[1]
    n_heads, nope = w_uk.shape[2], w_uk.shape[3]
    rope = w_uq.shape[3] - nope
    v_dim = w_uv.shape[3]
    half = rope // 2
    page = cache_latent.shape[2]
    past_len = page_table.shape[1] * page
    alpha = (2 * depth) ** 0.25
    scale = (nope + rope) ** -0.5 * LOG2E
    dims = (conv_dim, q_rank, kv_rank, n_heads, nope, half, scale)
    pad = HEAD_PAD - nope - rope
    assert pad >= 0 and 2 * v_dim == HEAD_PAD

    tabs_p = _rope_tables(jnp.arange(seq), rope, nope)
    tabs_s = _rope_tables(jnp.full((ns,), past_len), rope, nope)

    xp = x_prompt
    xs = x_sample.reshape(1, ns, d)
    c_all = jnp.concatenate([c_prompt, c_sample], axis=0)
    c_all = jnp.pad(c_all, ((0, -(nb + ns) % (2 * SUBLANES)), (0, 0)))
    outs = [[] for _ in range(6)]
    for l in range(depth):
        w_in_l = w_in[l]
        c5 = 3 * conv_dim + q_rank + kv_rank
        wmain = w_in_l[:, :c5].astype(BF16)
        wkr = jnp.pad(w_in_l[:, c5:], ((0, 0), (nope, pad))).astype(BF16)
        wq = jnp.pad(w_uq[l], ((0, 0), (0, 0), (0, pad))).reshape(q_rank, n_heads * HEAD_PAD).astype(BF16)
        wk = jnp.pad(w_uk[l], ((0, 0), (0, 0), (0, HEAD_PAD - nope))).reshape(kv_rank, n_heads * HEAD_PAD)
        wk = wk.astype(BF16)
        wv = w_uv[l].reshape(kv_rank, n_heads * v_dim).astype(BF16)
        wukt = jnp.transpose(w_uk[l], (1, 2, 0)).astype(BF16)
        eye = jnp.eye(n_heads, dtype=F32)
        wuv_bd = (w_uv[l].transpose(1, 0, 2)[:, :, None, :] * eye[:, None, :, None])
        wuv_bd = wuv_bd.reshape(n_heads * kv_rank, n_heads * v_dim).astype(BF16)
        woc = w_o[l, :conv_dim].astype(BF16)
        woa = w_o[l, conv_dim:].astype(BF16)
        proj_w = (wmain, wkr, conv_w[l], g_q[l].reshape(1, -1), g_kv[l].reshape(1, -1), wq)
        post_w = (woc, woa, ln1_g[l].reshape(1, -1), ln1_b[l].reshape(1, -1), w_gate[l].astype(BF16),
                  w_up[l].astype(BF16), w_down[l].astype(BF16), ln2_g[l].reshape(1, -1),
                  ln2_b[l].reshape(1, -1))

        mod = _ada(c_all, w_ada[l], b_ada[l])
        mp = [mod[:nb, i * d:(i + 1) * d].reshape(nb, 1, d) for i in range(6)]
        ms = [mod[nb:nb + ns, i * d:(i + 1) * d].reshape(1, ns, d) for i in range(6)]

        convout, q, k, v, lat_p, kr_p, tail = _proj(dims, False, 512, xp, mp[0], mp[1], tabs_p, proj_w,
                                                     (wk, wv))
        attn = _attn_prompt(q, k, v, n_heads)
        xp = _post(alpha, False, 512, xp, convout, attn, (mp[2], mp[3], mp[4], mp[5]), post_w, ())

        st = state_conv[l]
        convout_s, q_s, qlat_s, lat_s, kr_s, cin_s = _proj(
            dims, True, ns, xs, ms[0], ms[1], tabs_s, proj_w, (st[:, 0], st[:, 1], wukt))
        rows_pad = ((0, 0), (0, 2 * SUBLANES - n_heads), (0, 0))
        q16 = jnp.pad(q_s.reshape(ns, n_heads, HEAD_PAD), rows_pad)
        qlat16 = jnp.pad(qlat_s.reshape(ns, n_heads, kv_rank), rows_pad)
        olat = _attn_sample(page_table, q16, qlat16, lat_s.reshape(ns, 1, kv_rank),
                            kr_s.reshape(ns, 1, rope), cache_latent[l], cache_k_rope[l], n_heads, nope)
        xs = _post(alpha, True, ns, xs, convout_s, olat.reshape(1, ns, n_heads * kv_rank),
                   (ms[2], ms[3], ms[4], ms[5]), post_w, (wuv_bd,))

        outs[0].append(lat_p)
        outs[1].append(kr_p)
        outs[2].append(tail[:, SUBLANES - 2:, :])
        outs[3].append(lat_s.reshape(ns, 1, kv_rank))
        outs[4].append(kr_s.reshape(ns, 1, rope))
        outs[5].append(jnp.stack([st[:, 1], cin_s[0]], axis=1))
    return (xp, xs.reshape(ns, 1, d), jnp.stack(outs[0]), jnp.stack(outs[1]), jnp.stack(outs[2]),
            jnp.stack(outs[3]), jnp.stack(outs[4]), jnp.stack(outs[5]))
```

```python
import functools
import math

import jax
import jax.numpy as jnp
from jax import lax
from jax.experimental import pallas as pl
from jax.experimental.pallas import tpu as pltpu

F32 = jnp.float32
BF16 = jnp.bfloat16

ROPE_THETA = 10000.0
LN_EPS = 1e-5
RMS_EPS = 1e-6
NEG_INF = -1e30
LOG2E = math.log2(math.e)

LANES = 128
SUBLANES = 8
HEAD_PAD = 128
VMEM_LIMIT = 56 * 1024 * 1024


def _dot(a, b):
    return jnp.dot(a, b, preferred_element_type=F32)


def _dot_nt(a, b):
    return lax.dot_general(a, b, (((1,), (1,)), ((), ())), preferred_element_type=F32)


def _layer_norm(r, g, b):
    mu = jnp.mean(r, axis=-1, keepdims=True)
    d = r - mu
    var = jnp.mean(d * d, axis=-1, keepdims=True)
    return d * lax.rsqrt(var + LN_EPS) * g + b


def _rms_norm(x, g):
    return x * lax.rsqrt(jnp.mean(x * x, axis=-1, keepdims=True) + RMS_EPS) * g


def _rope_group(x, ta, tb, tc, half):
    return x * ta + pltpu.roll(x, half, 1) * tb + pltpu.roll(x, LANES - half, 1) * tc


def _ada_body(c_ref, w_ref, b_ref, o_ref):
    c = c_ref[...]
    s = c * jax.nn.sigmoid(c)
    o_ref[...] = _dot(s.astype(BF16), w_ref[...].astype(BF16)) + b_ref[...]


def _ada(c_all, w_ada, b_ada, tn=1024):
    m, d = c_all.shape
    n = w_ada.shape[1]
    return pl.pallas_call(
        _ada_body,
        grid=(n // tn,),
        in_specs=[pl.BlockSpec((m, d), lambda j: (0, 0)),
                  pl.BlockSpec((d, tn), lambda j: (0, j)),
                  pl.BlockSpec((1, tn), lambda j: (0, j))],
        out_specs=pl.BlockSpec((m, tn), lambda j: (0, j)),
        out_shape=jax.ShapeDtypeStruct((m, n), F32),
        compiler_params=pltpu.CompilerParams(dimension_semantics=("arbitrary",),
                                             vmem_limit_bytes=VMEM_LIMIT),
        name="ada",
    )(c_all, w_ada, b_ada.reshape(1, n))


def _proj_body(dims, sample, *refs):
    conv_dim, q_rank, kv_rank, n_heads, nope, half, scale = dims
    if sample:
        (x_ref, sh_ref, sc_ref, ta_ref, tb_ref, tc_ref, wmain_ref, wkr_ref, convw_ref, gq_ref, gkv_ref,
         wq_ref, s0_ref, s1_ref, wukt_ref,
         convout_ref, q_ref, qlat_ref, lat_ref, krot_ref, cin_ref) = refs
    else:
        (x_ref, sh_ref, sc_ref, ta_ref, tb_ref, tc_ref, wmain_ref, wkr_ref, convw_ref, gq_ref, gkv_ref,
         wq_ref, wk_ref, wv_ref,
         convout_ref, q_ref, k_ref, v_ref, lat_ref, krot_ref, tail_ref, cin_buf) = refs
    tm = x_ref.shape[1]
    c1, c2, c3 = conv_dim, 2 * conv_dim, 3 * conv_dim
    c4 = c3 + q_rank
    c5 = c4 + kv_rank

    x = x_ref[0]
    u = x * (1.0 + sc_ref[0]) + sh_ref[0]
    ub = u.astype(BF16)
    h = _dot(ub, wmain_ref[:, 0:c1])
    gb = _dot(ub, wmain_ref[:, c1:c2])
    gc = _dot(ub, wmain_ref[:, c2:c3])
    cq = _dot(ub, wmain_ref[:, c3:c4])
    ckv = _dot(ub, wmain_ref[:, c4:c5])
    kr = _dot(ub, wkr_ref[...])

    conv_in = gc * h
    w0 = convw_ref[0:1, :]
    w1 = convw_ref[1:2, :]
    w2 = convw_ref[2:3, :]
    if sample:
        y = w0 * s0_ref[...] + w1 * s1_ref[...] + w2 * conv_in
        cin_ref[0] = conv_in
    else:
        @pl.when(pl.program_id(1) == 0)
        def _():
            cin_buf[0:SUBLANES, :] = jnp.zeros((SUBLANES, conv_dim), F32)
        cin_buf[SUBLANES:SUBLANES + tm, :] = conv_in
        y = (w0 * cin_buf[SUBLANES - 2:SUBLANES - 2 + tm, :]
             + w1 * cin_buf[SUBLANES - 1:SUBLANES - 1 + tm, :] + w2 * conv_in)
        tail = conv_in[tm - SUBLANES:tm, :]
        cin_buf[0:SUBLANES, :] = tail
        tail_ref[0] = tail
    convout_ref[0] = (gb * y).astype(BF16)

    ta = ta_ref[...]
    tb = tb_ref[...]
    tc = tc_ref[...]
    cqn = _rms_norm(cq, gq_ref[...]).astype(BF16)
    qf = _dot(cqn, wq_ref[...])
    q_heads = [_rope_group(qf[:, HEAD_PAD * i:HEAD_PAD * (i + 1)], ta, tb, tc, half) * scale
               for i in range(n_heads)]
    q_ref[0] = jnp.concatenate(q_heads, axis=1).astype(q_ref.dtype)

    latent = _rms_norm(ckv, gkv_ref[...])
    lat_ref[0] = latent
    krot = _rope_group(kr, ta, tb, tc, half)
    krot_ref[0] = krot[:, nope:nope + 2 * half]

    if sample:
        for i in range(n_heads):
            qn = q_heads[i][:, 0:nope].astype(BF16)
            qlat_ref[0, :, kv_rank * i:kv_rank * (i + 1)] = _dot(qn, wukt_ref[i])
    else:
        latb = latent.astype(BF16)
        kf = _dot(latb, wk_ref[...])
        k_ref[0] = (kf + jnp.concatenate([krot] * n_heads, axis=1)).astype(BF16)
        v_ref[0] = _dot(latb, wv_ref[...]).astype(BF16)


def _const_spec(shape):
    nd = len(shape)
    return pl.BlockSpec(shape, lambda *_: (0,) * nd, pipeline_mode=pl.Buffered(1))


def _proj(dims, sample, tm, x, sh, sc, tabs, weights, extra):
    conv_dim, q_rank, kv_rank, n_heads, nope, half, _ = dims
    nb, s, d = x.shape
    r = sh.shape[1]
    grid = (nb, s // tm)
    row = lambda b, i: (b, i, 0)
    mod_spec = pl.BlockSpec((1, r, d), (lambda b, i: (b, 0, 0)))
    tab_spec = pl.BlockSpec((tm, LANES), lambda b, i: (i, 0))
    wmain, wkr, convw, gq, gkv, wq = weights
    in_specs = [pl.BlockSpec((1, tm, d), row), mod_spec, mod_spec, tab_spec, tab_spec, tab_spec,
                _const_spec(wmain.shape), _const_spec(wkr.shape), _const_spec(convw.shape),
                _const_spec(gq.shape), _const_spec(gkv.shape), _const_spec(wq.shape)]
    in_specs += [_const_spec(e.shape) for e in extra]
    hq = n_heads * HEAD_PAD
    out_shape = [jax.ShapeDtypeStruct((nb, s, conv_dim), BF16)]
    out_specs = [pl.BlockSpec((1, tm, conv_dim), row)]
    if sample:
        out_shape += [jax.ShapeDtypeStruct((nb, s, hq), F32),
                      jax.ShapeDtypeStruct((nb, s, n_heads * kv_rank), F32)]
        out_specs += [pl.BlockSpec((1, tm, hq), row), pl.BlockSpec((1, tm, n_heads * kv_rank), row)]
    else:
        out_shape += [jax.ShapeDtypeStruct((nb, s, hq), BF16), jax.ShapeDtypeStruct((nb, s, hq), BF16),
                      jax.ShapeDtypeStruct((nb, s, n_heads * (hq // n_heads // 2)), BF16)]
        out_specs += [pl.BlockSpec((1, tm, hq), row), pl.BlockSpec((1, tm, hq), row),
                      pl.BlockSpec((1, tm, n_heads * (hq // n_heads // 2)), row)]
    out_shape += [jax.ShapeDtypeStruct((nb, s, kv_rank), F32), jax.ShapeDtypeStruct((nb, s, 2 * half), F32)]
    out_specs += [pl.BlockSpec((1, tm, kv_rank), row), pl.BlockSpec((1, tm, 2 * half), row)]
    scratch = []
    if sample:
        out_shape += [jax.ShapeDtypeStruct((nb, s, conv_dim), F32)]
        out_specs += [pl.BlockSpec((1, tm, conv_dim), row)]
    else:
        out_shape += [jax.ShapeDtypeStruct((nb, SUBLANES, conv_dim), F32)]
        out_specs += [pl.BlockSpec((1, SUBLANES, conv_dim), lambda b, i: (b, 0, 0))]
        scratch = [pltpu.VMEM((SUBLANES + tm, conv_dim), F32)]
    return pl.pallas_call(
        functools.partial(_proj_body, dims, sample),
        grid=grid, in_specs=in_specs, out_specs=out_specs, out_shape=out_shape,
        scratch_shapes=scratch,
        compiler_params=pltpu.CompilerParams(dimension_semantics=("arbitrary", "arbitrary"),
                                             vmem_limit_bytes=VMEM_LIMIT),
        name="proj_sample" if sample else "proj_prompt",
    )(x, sh, sc, *tabs, *weights, *extra)


def _attn_prompt_body(v_dim, q_ref, k_ref, v_ref, o_ref):
    tq = q_ref.shape[1]
    tk = tq
    qi = pl.program_id(2)
    rows = lax.broadcasted_iota(jnp.int32, (tq, tk), 0)
    cols = lax.broadcasted_iota(jnp.int32, (tq, tk), 1)
    outs = []
    for hh in range(2):
        q = q_ref[0, :, HEAD_PAD * hh:HEAD_PAD * (hh + 1)]

        def step(j, carry, masked):
            m, l, acc = carry
            k = k_ref[0, pl.ds(pl.multiple_of(j * tk, tk), tk), HEAD_PAD * hh:HEAD_PAD * (hh + 1)]
            v = v_ref[0, pl.ds(pl.multiple_of(j * tk, tk), tk), v_dim * hh:v_dim * (hh + 1)]
            s = _dot_nt(q, k)
            if masked:
                s = jnp.where(cols <= rows, s, NEG_INF)
            m_new = jnp.maximum(m, jnp.max(s, axis=-1, keepdims=True))
            p = jnp.exp2(s - m_new)
            alpha = jnp.exp2(m - m_new)
            l = alpha * l + jnp.sum(p, axis=-1, keepdims=True)
            acc = alpha * acc + _dot(p.astype(BF16), v)
            return m_new, l, acc

        init = (jnp.full((tq, 1), NEG_INF, F32), jnp.zeros((tq, 1), F32), jnp.zeros((tq, v_dim), F32))
        carry = lax.fori_loop(0, qi, functools.partial(step, masked=False), init)
        m, l, acc = step(qi, carry, True)
        outs.append(acc / l)
    o_ref[0] = jnp.concatenate(outs, axis=1).astype(o_ref.dtype)


def _attn_prompt(q, k, v, n_heads, tq=512):
    b, s, _ = q.shape
    v_dim = v.shape[2] // n_heads
    return pl.pallas_call(
        functools.partial(_attn_prompt_body, v_dim),
        grid=(b, n_heads // 2, s // tq),
        in_specs=[pl.BlockSpec((1, tq, 2 * HEAD_PAD), lambda bi, hp, qi: (bi, qi, hp)),
                  pl.BlockSpec((1, s, 2 * HEAD_PAD), lambda bi, hp, qi: (bi, 0, hp)),
                  pl.BlockSpec((1, s, 2 * v_dim), lambda bi, hp, qi: (bi, 0, hp))],
        out_specs=pl.BlockSpec((1, tq, 2 * v_dim), lambda bi, hp, qi: (bi, qi, hp)),
        out_shape=jax.ShapeDtypeStruct((b, s, n_heads * v_dim), BF16),
        compiler_params=pltpu.CompilerParams(dimension_semantics=("arbitrary",) * 3,
                                             vmem_limit_bytes=VMEM_LIMIT),
        name="attn_prompt",
    )(q, k, v)


def _attn_sample_body(geom, pt_ref, q_ref, qlat_ref, latn_ref, krn_ref, lat_hbm, krt_hbm, o_ref,
                      lat_buf, kr_buf, kb_buf, st_ref, acc_ref, p_ref, sems):
    nseq, n_pages, ch, page, nope, rope = geom
    n_chunks = n_pages // ch
    n_steps = nseq * n_chunks
    n_heads = o_ref.shape[1]
    hp = q_ref.shape[1]

    def page_copies(pg, slot, j):
        cols = pl.ds(j * page, page)
        return (pltpu.make_async_copy(lat_hbm.at[pg], lat_buf.at[slot, cols], sems.at[0, slot]),
                pltpu.make_async_copy(krt_hbm.at[pg], kr_buf.at[slot, :, cols], sems.at[1, slot]))

    def start_chunk(g, slot):
        for j in range(ch):
            for cp in page_copies(pt_ref[g * ch + j], slot, j):
                cp.start()

    def wait_chunk(slot):
        pltpu.make_async_copy(lat_buf.at[slot], lat_buf.at[slot], sems.at[0, slot]).wait()
        pltpu.make_async_copy(kr_buf.at[slot], kr_buf.at[slot], sems.at[1, slot]).wait()

    def add_chunk(acc, alpha, p, slot):
        return alpha * acc + _dot(p, kb_buf[slot])

    def substep(g, slot, state):
        m, l, alpha, acc, p = state
        wait_chunk(slot)
        acc = add_chunk(acc, alpha, p, 1 - slot)
        start_chunk(jnp.minimum(g + 1, n_steps - 1), 1 - slot)
        seq = lax.div(g, n_chunks)
        first = lax.rem(g, n_chunks) == 0

        kb = lat_buf[slot].astype(BF16)
        kb_buf[slot] = kb
        krb = kr_buf[slot].astype(BF16)
        q = q_ref[seq]
        qr = q[:, nope:nope + rope]
        qlat = qlat_ref[seq]
        latn = latn_ref[pl.ds(seq, 1), :]
        krn = krn_ref[pl.ds(seq, 1), :]
        s = _dot_nt(qlat.astype(BF16), kb) + _dot(qr.astype(BF16), krb)

        prev_seq = lax.div(jnp.maximum(g - 1, 0), n_chunks)
        o_ref[prev_seq] = (acc / l)[0:n_heads, :]

        s_new = (jnp.sum(qlat * latn, axis=-1, keepdims=True)
                 + jnp.sum(qr * krn, axis=-1, keepdims=True))
        m = jnp.where(first, s_new, m)
        l = jnp.where(first, 1.0, l)
        acc = jnp.where(first, jnp.broadcast_to(latn, acc.shape), acc)

        m_new = jnp.maximum(m, jnp.max(s, axis=-1, keepdims=True))
        p = jnp.exp2(s - m_new)
        alpha = jnp.exp2(m - m_new)
        l = alpha * l + jnp.sum(p, axis=-1, keepdims=True)
        return m_new, l, alpha, acc, p.astype(BF16)

    def put(i, col):
        st_ref[i] = jnp.broadcast_to(col, st_ref.shape[1:])

    i = pl.program_id(0)

    @pl.when(i == 0)
    def _():
        start_chunk(i, 0)
        kb_buf[1] = jnp.zeros(kb_buf.shape[1:], BF16)
        p_ref[...] = jnp.zeros(p_ref.shape, BF16)
        acc_ref[...] = jnp.zeros(acc_ref.shape, F32)
        put(0, jnp.zeros((hp, 1), F32))
        put(1, jnp.ones((hp, 1), F32))
        put(2, jnp.zeros((hp, 1), F32))

    state = (st_ref[0][:, 0:1], st_ref[1][:, 0:1], st_ref[2][:, 0:1], acc_ref[...], p_ref[...])
    state = substep(2 * i, 0, state)
    m, l, alpha, acc, p = substep(2 * i + 1, 1, state)
    put(0, m)
    put(1, l)
    put(2, alpha)
    acc_ref[...] = acc
    p_ref[...] = p

    @pl.when(i == pl.num_programs(0) - 1)
    def _():
        wait_chunk(0)
        o_ref[nseq - 1] = (add_chunk(acc, alpha, p, 1) / l)[0:n_heads, :]


def _attn_sample(page_table, q16, qlat16, lat_new, kr_new, cache_lat, cache_krt, n_heads, nope, ch=32):
    nseq, n_pages = page_table.shape
    _, page, kv_rank = cache_lat.shape
    rope = cache_krt.shape[1]
    assert n_pages % ch == 0 and (nseq * (n_pages // ch)) % 2 == 0
    chk = ch * page
    hp = q16.shape[1]
    geom = (nseq, n_pages, ch, page, nope, rope)
    whole = lambda a: pl.BlockSpec(a.shape, lambda i, pt: (0,) * a.ndim)
    grid_spec = pltpu.PrefetchScalarGridSpec(
        num_scalar_prefetch=1,
        grid=(nseq * (n_pages // ch) // 2,),
        in_specs=[whole(q16), whole(qlat16), whole(lat_new), whole(kr_new),
                  pl.BlockSpec(memory_space=pl.ANY), pl.BlockSpec(memory_space=pl.ANY)],
        out_specs=pl.BlockSpec((nseq, n_heads, kv_rank), lambda i, pt: (0, 0, 0)),
        scratch_shapes=[pltpu.VMEM((2, chk, kv_rank), F32),
                        pltpu.VMEM((2, rope, chk), F32),
                        pltpu.VMEM((2, chk, kv_rank), BF16),
                        pltpu.VMEM((3, hp, LANES), F32),
                        pltpu.VMEM((hp, kv_rank), F32),
                        pltpu.VMEM((hp, chk), BF16),
                        pltpu.SemaphoreType.DMA((2, 2))],
    )
    return pl.pallas_call(
        functools.partial(_attn_sample_body, geom),
        grid_spec=grid_spec,
        out_shape=jax.ShapeDtypeStruct((nseq, n_heads, kv_rank), F32),
        compiler_params=pltpu.CompilerParams(dimension_semantics=("arbitrary",),
                                             vmem_limit_bytes=VMEM_LIMIT),
        name="attn_sample",
    )(page_table.reshape(-1), q16, qlat16, lat_new, kr_new, cache_lat, cache_krt)


def _post_body(alpha, ff_chunks, sample, *refs):
    if sample:
        (x_ref, convout_ref, olat_ref, ga_ref, shf_ref, scf_ref, gf_ref, woc_ref, woa_ref,
         ln1g_ref, ln1b_ref, wg_ref, wu_ref, wd_ref, ln2g_ref, ln2b_ref, wuv_ref, y_ref) = refs
        attn = _dot(olat_ref[0].astype(BF16), wuv_ref[...]).astype(BF16)
    else:
        (x_ref, convout_ref, attn_ref, ga_ref, shf_ref, scf_ref, gf_ref, woc_ref, woa_ref,
         ln1g_ref, ln1b_ref, wg_ref, wu_ref, wd_ref, ln2g_ref, ln2b_ref, y_ref) = refs
        attn = attn_ref[0]
    x = x_ref[0]
    a = _dot(convout_ref[0], woc_ref[...]) + _dot(attn, woa_ref[...])
    x1 = _layer_norm(alpha * x + (1.0 + ga_ref[0]) * a, ln1g_ref[...], ln1b_ref[...])
    ub = (x1 * (1.0 + scf_ref[0]) + shf_ref[0]).astype(BF16)
    f = None
    for lo, hi in ff_chunks:
        g = _dot(ub, wg_ref[:, lo:hi])
        up = _dot(ub, wu_ref[:, lo:hi])
        hmid = (g * jax.nn.sigmoid(g) * up).astype(BF16)
        part = _dot(hmid, wd_ref[lo:hi, :])
        f = part if f is None else f + part
    y_ref[0] = _layer_norm(alpha * x1 + (1.0 + gf_ref[0]) * f, ln2g_ref[...], ln2b_ref[...])


def _post(alpha, sample, tm, x, convout, attn, mods, weights, extra):
    nb, s, d = x.shape
    r = mods[0].shape[1]
    d_ff = weights[5].shape[1]
    n_ff = 2 if d_ff % (2 * LANES) == 0 else 1
    ff_chunks = tuple((i * d_ff // n_ff, (i + 1) * d_ff // n_ff) for i in range(n_ff))
    row = lambda b, i: (b, i, 0)
    mod_spec = pl.BlockSpec((1, r, d), lambda b, i: (b, 0, 0))
    in_specs = [pl.BlockSpec((1, tm, d), row),
                pl.BlockSpec((1, tm, convout.shape[2]), row),
                pl.BlockSpec((1, tm, attn.shape[2]), row),
                mod_spec, mod_spec, mod_spec, mod_spec]
    in_specs += [_const_spec(w.shape) for w in weights]
    in_specs += [_const_spec(e.shape) for e in extra]
    return pl.pallas_call(
        functools.partial(_post_body, alpha, ff_chunks, sample),
        grid=(nb, s // tm), in_specs=in_specs,
        out_specs=pl.BlockSpec((1, tm, d), row),
        out_shape=jax.ShapeDtypeStruct((nb, s, d), F32),
        compiler_params=pltpu.CompilerParams(dimension_semantics=("arbitrary", "arbitrary"),
                                             vmem_limit_bytes=VMEM_LIMIT),
        name="post_sample" if sample else "post_prompt",
    )(x, convout, attn, *mods, *weights, *extra)


def _rope_tables(pos, rope, nope):
    half = rope // 2
    inv = 1.0 / (ROPE_THETA ** (jnp.arange(0, rope, 2, dtype=F32) / rope))
    ang = pos.astype(F32)[:, None] * inv[None, :]
    cos, sin = jnp.cos(ang), jnp.sin(ang)
    n = pos.shape[0]
    z = lambda w: jnp.zeros((n, w), F32)
    pad = LANES - nope - rope
    ta = jnp.concatenate([jnp.ones((n, nope), F32), cos, cos, z(pad)], axis=1)
    tb = jnp.concatenate([z(nope + half), sin, z(pad)], axis=1)
    tc = jnp.concatenate([z(nope), -sin, z(half + pad)], axis=1)
    return ta, tb, tc


def kernel(x_prompt, x_sample, cache_latent, cache_k_rope, state_conv, page_table, c_prompt, c_sample,
           w_ada, b_ada, w_in, conv_w, g_q, g_kv, w_uq, w_uk, w_uv, w_o, ln1_g, ln1_b, w_gate, w_up,
           w_down, ln2_g, ln2_b):
    depth = w_ada.shape[0]
    nb, seq, d = x_prompt.shape
    ns, dec_seq, _ = x_sample.shape
    assert dec_seq == 1
    conv_dim = conv_w.shape[2]
    q_rank = g_q.shape[1]
    kv_rank = g_kv.shape[1]
    n_heads, nope = w_uk.shape[2], w_uk.shape[3]
    rope = w_uq.shape[3] - nope
    v_dim = w_uv.shape[3]
    half = rope // 2
    page = cache_latent.shape[2]
    past_len = page_table.shape[1] * page
    alpha = (2 * depth) ** 0.25
    scale = (nope + rope) ** -0.5 * LOG2E
    dims = (conv_dim, q_rank, kv_rank, n_heads, nope, half, scale)
    pad = HEAD_PAD - nope - rope
    assert pad >= 0 and 2 * v_dim == HEAD_PAD

    tabs_p = _rope_tables(jnp.arange(seq), rope, nope)
    tabs_s = _rope_tables(jnp.full((ns,), past_len), rope, nope)

    xp = x_prompt
    xs = x_sample.reshape(1, ns, d)
    c_all = jnp.concatenate([c_prompt, c_sample], axis=0)
    c_all = jnp.pad(c_all, ((0, -(nb + ns) % (2 * SUBLANES)), (0, 0)))
    outs = [[] for _ in range(6)]
    for l in range(depth):
        w_in_l = w_in[l]
        c5 = 3 * conv_dim + q_rank + kv_rank
        wmain = w_in_l[:, :c5].astype(BF16)
        wkr = jnp.pad(w_in_l[:, c5:], ((0, 0), (nope, pad))).astype(BF16)
        wq = jnp.pad(w_uq[l], ((0, 0), (0, 0), (0, pad))).reshape(q_rank, n_heads * HEAD_PAD).astype(BF16)
        wk = jnp.pad(w_uk[l], ((0, 0), (0, 0), (0, HEAD_PAD - nope))).reshape(kv_rank, n_heads * HEAD_PAD)
        wk = wk.astype(BF16)
        wv = w_uv[l].reshape(kv_rank, n_heads * v_dim).astype(BF16)
        wukt = jnp.transpose(w_uk[l], (1, 2, 0)).astype(BF16)
        eye = jnp.eye(n_heads, dtype=F32)
        wuv_bd = (w_uv[l].transpose(1, 0, 2)[:, :, None, :] * eye[:, None, :, None])
        wuv_bd = wuv_bd.reshape(n_heads * kv_rank, n_heads * v_dim).astype(BF16)
        woc = w_o[l, :conv_dim].astype(BF16)
        woa = w_o[l, conv_dim:].astype(BF16)
        proj_w = (wmain, wkr, conv_w[l], g_q[l].reshape(1, -1), g_kv[l].reshape(1, -1), wq)
        post_w = (woc, woa, ln1_g[l].reshape(1, -1), ln1_b[l].reshape(1, -1), w_gate[l].astype(BF16),
                  w_up[l].astype(BF16), w_down[l].astype(BF16), ln2_g[l].reshape(1, -1),
                  ln2_b[l].reshape(1, -1))

        mod = _ada(c_all, w_ada[l], b_ada[l])
        mp = [mod[:nb, i * d:(i + 1) * d].reshape(nb, 1, d) for i in range(6)]
        ms = [mod[nb:nb + ns, i * d:(i + 1) * d].reshape(1, ns, d) for i in range(6)]

        convout, q, k, v, lat_p, kr_p, tail = _proj(dims, False, 512, xp, mp[0], mp[1], tabs_p, proj_w,
                                                     (wk, wv))
        attn = _attn_prompt(q, k, v, n_heads)
        xp = _post(alpha, False, 512, xp, convout, attn, (mp[2], mp[3], mp[4], mp[5]), post_w, ())

        st = state_conv[l]
        convout_s, q_s, qlat_s, lat_s, kr_s, cin_s = _proj(
            dims, True, ns, xs, ms[0], ms[1], tabs_s, proj_w, (st[:, 0], st[:, 1], wukt))
        rows_pad = ((0, 0), (0, 2 * SUBLANES - n_heads), (0, 0))
        q16 = jnp.pad(q_s.reshape(ns, n_heads, HEAD_PAD), rows_pad)
        qlat16 = jnp.pad(qlat_s.reshape(ns, n_heads, kv_rank), rows_pad)
        cache_krt = jnp.swapaxes(cache_k_rope[l], 1, 2)
        olat = _attn_sample(page_table, q16, qlat16, lat_s[0], kr_s[0], cache_latent[l], cache_krt,
                            n_heads, nope)
        xs = _post(alpha, True, ns, xs, convout_s, olat.reshape(1, ns, n_heads * kv_rank),
                   (ms[2], ms[3], ms[4], ms[5]), post_w, (wuv_bd,))

        outs[0].append(lat_p)
        outs[1].append(kr_p)
        outs[2].append(tail[:, SUBLANES - 2:, :])
        outs[3].append(lat_s.reshape(ns, 1, kv_rank))
        outs[4].append(kr_s.reshape(ns, 1, rope))
        outs[5].append(jnp.stack([st[:, 1], cin_s[0]], axis=1))
    return (xp, xs.reshape(ns, 1, d), jnp.stack(outs[0]), jnp.stack(outs[1]), jnp.stack(outs[2]),
            jnp.stack(outs[3]), jnp.stack(outs[4]), jnp.stack(outs[5]))
```

```python
import functools
import math

import jax
import jax.numpy as jnp
from jax import lax
from jax.experimental import pallas as pl
from jax.experimental.pallas import tpu as pltpu

F32 = jnp.float32
BF16 = jnp.bfloat16

ROPE_THETA = 10000.0
LN_EPS = 1e-5
RMS_EPS = 1e-6
NEG_INF = -1e30
LOG2E = math.log2(math.e)

LANES = 128
SUBLANES = 8
HEAD_PAD = 128
VMEM_LIMIT = 56 * 1024 * 1024


def _dot(a, b):
    return jnp.dot(a, b, preferred_element_type=F32)


def _dot_nt(a, b):
    return lax.dot_general(a, b, (((1,), (1,)), ((), ())), preferred_element_type=F32)


def _layer_norm(r, g, b):
    mu = jnp.mean(r, axis=-1, keepdims=True)
    d = r - mu
    var = jnp.mean(d * d, axis=-1, keepdims=True)
    return d * lax.rsqrt(var + LN_EPS) * g + b


def _rms_norm(x, g):
    return x * lax.rsqrt(jnp.mean(x * x, axis=-1, keepdims=True) + RMS_EPS) * g


def _rope_group(x, ta, tb, tc, half):
    return x * ta + pltpu.roll(x, half, 1) * tb + pltpu.roll(x, LANES - half, 1) * tc


def _ada_body(c_ref, w_ref, b_ref, o_ref):
    c = c_ref[...]
    s = c * jax.nn.sigmoid(c)
    o_ref[...] = _dot(s.astype(BF16), w_ref[...].astype(BF16)) + b_ref[...]


def _ada(c_all, w_ada, b_ada, tn=1024):
    m, d = c_all.shape
    n = w_ada.shape[1]
    return pl.pallas_call(
        _ada_body,
        grid=(n // tn,),
        in_specs=[pl.BlockSpec((m, d), lambda j: (0, 0)),
                  pl.BlockSpec((d, tn), lambda j: (0, j)),
                  pl.BlockSpec((1, tn), lambda j: (0, j))],
        out_specs=pl.BlockSpec((m, tn), lambda j: (0, j)),
        out_shape=jax.ShapeDtypeStruct((m, n), F32),
        compiler_params=pltpu.CompilerParams(dimension_semantics=("arbitrary",),
                                             vmem_limit_bytes=VMEM_LIMIT),
        name="ada",
    )(c_all, w_ada, b_ada.reshape(1, n))


def _proj_body(dims, sample, *refs):
    conv_dim, q_rank, kv_rank, n_heads, nope, half, scale = dims
    if sample:
        (x_ref, sh_ref, sc_ref, ta_ref, tb_ref, tc_ref, wmain_ref, wkr_ref, convw_ref, gq_ref, gkv_ref,
         wq_ref, s0_ref, s1_ref, wukt_ref,
         convout_ref, q_ref, qlat_ref, lat_ref, krot_ref, cin_ref) = refs
    else:
        (x_ref, sh_ref, sc_ref, ta_ref, tb_ref, tc_ref, wmain_ref, wkr_ref, convw_ref, gq_ref, gkv_ref,
         wq_ref, wk_ref, wv_ref,
         convout_ref, q_ref, k_ref, v_ref, lat_ref, krot_ref, tail_ref, cin_buf) = refs
    tm = x_ref.shape[1]
    c1, c2, c3 = conv_dim, 2 * conv_dim, 3 * conv_dim
    c4 = c3 + q_rank
    c5 = c4 + kv_rank

    x = x_ref[0]
    u = x * (1.0 + sc_ref[0]) + sh_ref[0]
    ub = u.astype(BF16)
    h = _dot(ub, wmain_ref[:, 0:c1])
    gb = _dot(ub, wmain_ref[:, c1:c2])
    gc = _dot(ub, wmain_ref[:, c2:c3])
    cq = _dot(ub, wmain_ref[:, c3:c4])
    ckv = _dot(ub, wmain_ref[:, c4:c5])
    kr = _dot(ub, wkr_ref[...])

    conv_in = gc * h
    w0 = convw_ref[0:1, :]
    w1 = convw_ref[1:2, :]
    w2 = convw_ref[2:3, :]
    if sample:
        y = w0 * s0_ref[...] + w1 * s1_ref[...] + w2 * conv_in
        cin_ref[0] = conv_in
    else:
        @pl.when(pl.program_id(1) == 0)
        def _():
            cin_buf[0:SUBLANES, :] = jnp.zeros((SUBLANES, conv_dim), F32)
        cin_buf[SUBLANES:SUBLANES + tm, :] = conv_in
        y = (w0 * cin_buf[SUBLANES - 2:SUBLANES - 2 + tm, :]
             + w1 * cin_buf[SUBLANES - 1:SUBLANES - 1 + tm, :] + w2 * conv_in)
        tail = conv_in[tm - SUBLANES:tm, :]
        cin_buf[0:SUBLANES, :] = tail
        tail_ref[0] = tail
    convout_ref[0] = (gb * y).astype(BF16)

    ta = ta_ref[...]
    tb = tb_ref[...]
    tc = tc_ref[...]
    cqn = _rms_norm(cq, gq_ref[...]).astype(BF16)
    qf = _dot(cqn, wq_ref[...])
    q_heads = [_rope_group(qf[:, HEAD_PAD * i:HEAD_PAD * (i + 1)], ta, tb, tc, half) * scale
               for i in range(n_heads)]
    q_ref[0] = jnp.concatenate(q_heads, axis=1).astype(q_ref.dtype)

    latent = _rms_norm(ckv, gkv_ref[...])
    lat_ref[0] = latent
    krot = _rope_group(kr, ta, tb, tc, half)
    krot_ref[0] = krot[:, nope:nope + 2 * half]

    if sample:
        for i in range(n_heads):
            qn = q_heads[i][:, 0:nope].astype(BF16)
            qlat_ref[0, :, kv_rank * i:kv_rank * (i + 1)] = _dot(qn, wukt_ref[i])
    else:
        latb = latent.astype(BF16)
        kf = _dot(latb, wk_ref[...])
        k_ref[0] = (kf + jnp.concatenate([krot] * n_heads, axis=1)).astype(BF16)
        v_ref[0] = _dot(latb, wv_ref[...]).astype(BF16)


def _const_spec(shape):
    nd = len(shape)
    return pl.BlockSpec(shape, lambda *_: (0,) * nd, pipeline_mode=pl.Buffered(1))


def _proj(dims, sample, tm, x, sh, sc, tabs, weights, extra):
    conv_dim, q_rank, kv_rank, n_heads, nope, half, _ = dims
    nb, s, d = x.shape
    r = sh.shape[1]
    grid = (nb, s // tm)
    row = lambda b, i: (b, i, 0)
    mod_spec = pl.BlockSpec((1, r, d), (lambda b, i: (b, 0, 0)))
    tab_spec = pl.BlockSpec((tm, LANES), lambda b, i: (i, 0))
    wmain, wkr, convw, gq, gkv, wq = weights
    in_specs = [pl.BlockSpec((1, tm, d), row), mod_spec, mod_spec, tab_spec, tab_spec, tab_spec,
                _const_spec(wmain.shape), _const_spec(wkr.shape), _const_spec(convw.shape),
                _const_spec(gq.shape), _const_spec(gkv.shape), _const_spec(wq.shape)]
    in_specs += [_const_spec(e.shape) for e in extra]
    hq = n_heads * HEAD_PAD
    out_shape = [jax.ShapeDtypeStruct((nb, s, conv_dim), BF16)]
    out_specs = [pl.BlockSpec((1, tm, conv_dim), row)]
    if sample:
        out_shape += [jax.ShapeDtypeStruct((nb, s, hq), F32),
                      jax.ShapeDtypeStruct((nb, s, n_heads * kv_rank), F32)]
        out_specs += [pl.BlockSpec((1, tm, hq), row), pl.BlockSpec((1, tm, n_heads * kv_rank), row)]
    else:
        out_shape += [jax.ShapeDtypeStruct((nb, s, hq), BF16), jax.ShapeDtypeStruct((nb, s, hq), BF16),
                      jax.ShapeDtypeStruct((nb, s, n_heads * (hq // n_heads // 2)), BF16)]
        out_specs += [pl.BlockSpec((1, tm, hq), row), pl.BlockSpec((1, tm, hq), row),
                      pl.BlockSpec((1, tm, n_heads * (hq // n_heads // 2)), row)]
    out_shape += [jax.ShapeDtypeStruct((nb, s, kv_rank), F32), jax.ShapeDtypeStruct((nb, s, 2 * half), F32)]
    out_specs += [pl.BlockSpec((1, tm, kv_rank), row), pl.BlockSpec((1, tm, 2 * half), row)]
    scratch = []
    if sample:
        out_shape += [jax.ShapeDtypeStruct((nb, s, conv_dim), F32)]
        out_specs += [pl.BlockSpec((1, tm, conv_dim), row)]
    else:
        out_shape += [jax.ShapeDtypeStruct((nb, SUBLANES, conv_dim), F32)]
        out_specs += [pl.BlockSpec((1, SUBLANES, conv_dim), lambda b, i: (b, 0, 0))]
        scratch = [pltpu.VMEM((SUBLANES + tm, conv_dim), F32)]
    return pl.pallas_call(
        functools.partial(_proj_body, dims, sample),
        grid=grid, in_specs=in_specs, out_specs=out_specs, out_shape=out_shape,
        scratch_shapes=scratch,
        compiler_params=pltpu.CompilerParams(dimension_semantics=("arbitrary", "arbitrary"),
                                             vmem_limit_bytes=VMEM_LIMIT),
        name="proj_sample" if sample else "proj_prompt",
    )(x, sh, sc, *tabs, *weights, *extra)


def _attn_prompt_body(v_dim, q_ref, k_ref, v_ref, o_ref):
    tq = q_ref.shape[1]
    tk = tq
    qi = pl.program_id(2)
    rows = lax.broadcasted_iota(jnp.int32, (tq, tk), 0)
    cols = lax.broadcasted_iota(jnp.int32, (tq, tk), 1)
    outs = []
    for hh in range(2):
        q = q_ref[0, :, HEAD_PAD * hh:HEAD_PAD * (hh + 1)]

        def step(j, carry, masked):
            m, l, acc = carry
            k = k_ref[0, pl.ds(pl.multiple_of(j * tk, tk), tk), HEAD_PAD * hh:HEAD_PAD * (hh + 1)]
            v = v_ref[0, pl.ds(pl.multiple_of(j * tk, tk), tk), v_dim * hh:v_dim * (hh + 1)]
            s = _dot_nt(q, k)
            if masked:
                s = jnp.where(cols <= rows, s, NEG_INF)
            m_new = jnp.maximum(m, jnp.max(s, axis=-1, keepdims=True))
            p = jnp.exp2(s - m_new)
            alpha = jnp.exp2(m - m_new)
            l = alpha * l + jnp.sum(p, axis=-1, keepdims=True)
            acc = alpha * acc + _dot(p.astype(BF16), v)
            return m_new, l, acc

        init = (jnp.full((tq, 1), NEG_INF, F32), jnp.zeros((tq, 1), F32), jnp.zeros((tq, v_dim), F32))
        carry = lax.fori_loop(0, qi, functools.partial(step, masked=False), init)
        m, l, acc = step(qi, carry, True)
        outs.append(acc / l)
    o_ref[0] = jnp.concatenate(outs, axis=1).astype(o_ref.dtype)


def _attn_prompt(q, k, v, n_heads, tq=512):
    b, s, _ = q.shape
    v_dim = v.shape[2] // n_heads
    return pl.pallas_call(
        functools.partial(_attn_prompt_body, v_dim),
        grid=(b, n_heads // 2, s // tq),
        in_specs=[pl.BlockSpec((1, tq, 2 * HEAD_PAD), lambda bi, hp, qi: (bi, qi, hp)),
                  pl.BlockSpec((1, s, 2 * HEAD_PAD), lambda bi, hp, qi: (bi, 0, hp)),
                  pl.BlockSpec((1, s, 2 * v_dim), lambda bi, hp, qi: (bi, 0, hp))],
        out_specs=pl.BlockSpec((1, tq, 2 * v_dim), lambda bi, hp, qi: (bi, qi, hp)),
        out_shape=jax.ShapeDtypeStruct((b, s, n_heads * v_dim), BF16),
        compiler_params=pltpu.CompilerParams(dimension_semantics=("arbitrary",) * 3,
                                             vmem_limit_bytes=VMEM_LIMIT),
        name="attn_prompt",
    )(q, k, v)


def _attn_sample_body(geom, pt_ref, q_ref, qlat_ref, latn_ref, krn_ref, lat_hbm, krt_hbm, o_ref,
                      lat_buf, kr_buf, kb_buf, s_buf, st_ref, acc_ref, p_ref, sems):
    nseq, n_pages, ch, page, nope, rope = geom
    n_chunks = n_pages // ch
    n_steps = nseq * n_chunks
    n_heads = o_ref.shape[1]
    hp = q_ref.shape[1]
    n_slots = lat_buf.shape[0]
    n_kb = kb_buf.shape[0]

    def page_copies(pg, slot, j):
        cols = pl.ds(j * page, page)
        return (pltpu.make_async_copy(lat_hbm.at[pg], lat_buf.at[slot, cols], sems.at[0, slot]),
                pltpu.make_async_copy(krt_hbm.at[pg], kr_buf.at[slot, :, cols], sems.at[1, slot]))

    def start_chunk(g, slot):
        for j in range(ch):
            for cp in page_copies(pt_ref[g * ch + j], slot, j):
                cp.start()

    def wait_chunk(slot):
        pltpu.make_async_copy(lat_buf.at[slot], lat_buf.at[slot], sems.at[0, slot]).wait()
        pltpu.make_async_copy(kr_buf.at[slot], kr_buf.at[slot], sems.at[1, slot]).wait()

    def put(i, col):
        st_ref[i] = jnp.broadcast_to(col, st_ref.shape[1:])

    def substep(a, u):
        m, l, alpha = st_ref[0][:, 0:1], st_ref[1][:, 0:1], st_ref[2][:, 0:1]
        acc, p = acc_ref[...], p_ref[...]
        wait_chunk(u)

        acc = alpha * acc + _dot(p, kb_buf[(u - 2) % n_kb])
        out_row = jnp.minimum(lax.div(jnp.maximum(a - 2, 0), n_chunks), nseq)
        o_ref[out_row] = (acc / l)[0:n_heads, :]

        seq_a = jnp.minimum(lax.div(a, n_chunks), nseq - 1)
        refill = jnp.minimum(a + n_slots - 1, n_steps - 1) * ch
        rslot = (u + n_slots - 1) % n_slots
        pages = []
        for j in range(ch):
            rows = pl.ds(j * page, page)
            kbj = lat_buf[u, rows, :].astype(BF16)
            kb_buf[u % n_kb, rows, :] = kbj
            pages.append(kbj)
            for cp in page_copies(pt_ref[refill + j], rslot, j):
                cp.start()
        kb = jnp.concatenate(pages, axis=0)
        krb = kr_buf[u].astype(BF16)
        qa = q_ref[seq_a]
        s_buf[u % 2] = (_dot_nt(qlat_ref[seq_a].astype(BF16), kb)
                        + _dot(qa[:, nope:nope + rope].astype(BF16), krb))

        b = jnp.maximum(a - 1, 0)
        seq_b = jnp.minimum(lax.div(b, n_chunks), nseq - 1)
        first = lax.rem(b, n_chunks) == 0
        qb = q_ref[seq_b]
        qlat = qlat_ref[seq_b]
        latn = latn_ref[pl.ds(seq_b, 1), :]
        krn = krn_ref[pl.ds(seq_b, 1), :]
        s_new = (jnp.sum(qlat * latn, axis=-1, keepdims=True)
                 + jnp.sum(qb[:, nope:nope + rope] * krn, axis=-1, keepdims=True))
        m = jnp.where(first, s_new, m)
        l = jnp.where(first, 1.0, l)
        acc = jnp.where(first, jnp.broadcast_to(latn, acc.shape), acc)
        s = s_buf[(u - 1) % 2]
        m_new = jnp.maximum(m, jnp.max(s, axis=-1, keepdims=True))
        p = jnp.exp2(s - m_new)
        alpha = jnp.exp2(m - m_new)
        put(0, m_new)
        put(1, alpha * l + jnp.sum(p, axis=-1, keepdims=True))
        put(2, alpha)
        acc_ref[...] = acc
        p_ref[...] = p.astype(BF16)

    i = pl.program_id(0)

    @pl.when(i == 0)
    def _():
        for c in range(n_slots - 1):
            start_chunk(i + c, c)
        for k in range(2, n_kb):
            kb_buf[k] = jnp.zeros(kb_buf.shape[1:], BF16)
        s_buf[1] = jnp.zeros(s_buf.shape[1:], F32)
        p_ref[...] = jnp.zeros(p_ref.shape, BF16)
        acc_ref[...] = jnp.zeros(acc_ref.shape, F32)
        put(0, jnp.zeros((hp, 1), F32))
        put(1, jnp.ones((hp, 1), F32))
        put(2, jnp.zeros((hp, 1), F32))

    for u in range(n_slots):
        pl.when(i >= 0)(functools.partial(substep, n_slots * i + u, u))

    @pl.when(i == pl.num_programs(0) - 1)
    def _():
        for slot in range(n_slots - 1):
            wait_chunk(slot)


def _attn_sample(page_table, q16, qlat16, lat_new, kr_new, cache_lat, cache_krt, n_heads, nope,
                 ch=32, n_slots=4):
    nseq, n_pages = page_table.shape
    _, page, kv_rank = cache_lat.shape
    rope = cache_krt.shape[1]
    assert n_pages % ch == 0 and n_slots % 4 == 0
    chk = ch * page
    hp = q16.shape[1]
    geom = (nseq, n_pages, ch, page, nope, rope)
    positions = nseq * (n_pages // ch) + 2
    whole = lambda a: pl.BlockSpec(a.shape, lambda i, pt: (0,) * a.ndim)
    grid_spec = pltpu.PrefetchScalarGridSpec(
        num_scalar_prefetch=1,
        grid=(pl.cdiv(positions, n_slots),),
        in_specs=[whole(q16), whole(qlat16), whole(lat_new), whole(kr_new),
                  pl.BlockSpec(memory_space=pl.ANY), pl.BlockSpec(memory_space=pl.ANY)],
        out_specs=pl.BlockSpec((nseq + 1, n_heads, kv_rank), lambda i, pt: (0, 0, 0)),
        scratch_shapes=[pltpu.VMEM((n_slots, chk, kv_rank), F32),
                        pltpu.VMEM((n_slots, rope, chk), F32),
                        pltpu.VMEM((4, chk, kv_rank), BF16),
                        pltpu.VMEM((2, hp, chk), F32),
                        pltpu.VMEM((3, hp, LANES), F32),
                        pltpu.VMEM((hp, kv_rank), F32),
                        pltpu.VMEM((hp, chk), BF16),
                        pltpu.SemaphoreType.DMA((2, n_slots))],
    )
    out = pl.pallas_call(
        functools.partial(_attn_sample_body, geom),
        grid_spec=grid_spec,
        out_shape=jax.ShapeDtypeStruct((nseq + 1, n_heads, kv_rank), F32),
        compiler_params=pltpu.CompilerParams(dimension_semantics=("arbitrary",),
                                             vmem_limit_bytes=VMEM_LIMIT),
        name="attn_sample",
    )(page_table.reshape(-1), q16, qlat16, lat_new, kr_new, cache_lat, cache_krt)
    return out[:nseq]


def _post_body(alpha, ff_chunks, sample, *refs):
    if sample:
        (x_ref, convout_ref, olat_ref, ga_ref, shf_ref, scf_ref, gf_ref, woc_ref, woa_ref,
         ln1g_ref, ln1b_ref, wg_ref, wu_ref, wd_ref, ln2g_ref, ln2b_ref, wuv_ref, y_ref) = refs
        attn = _dot(olat_ref[0].astype(BF16), wuv_ref[...]).astype(BF16)
    else:
        (x_ref, convout_ref, attn_ref, ga_ref, shf_ref, scf_ref, gf_ref, woc_ref, woa_ref,
         ln1g_ref, ln1b_ref, wg_ref, wu_ref, wd_ref, ln2g_ref, ln2b_ref, y_ref) = refs
        attn = attn_ref[0]
    x = x_ref[0]
    a = _dot(convout_ref[0], woc_ref[...]) + _dot(attn, woa_ref[...])
    x1 = _layer_norm(alpha * x + (1.0 + ga_ref[0]) * a, ln1g_ref[...], ln1b_ref[...])
    ub = (x1 * (1.0 + scf_ref[0]) + shf_ref[0]).astype(BF16)
    f = None
    for lo, hi in ff_chunks:
        g = _dot(ub, wg_ref[:, lo:hi])
        up = _dot(ub, wu_ref[:, lo:hi])
        hmid = (g * jax.nn.sigmoid(g) * up).astype(BF16)
        part = _dot(hmid, wd_ref[lo:hi, :])
        f = part if f is None else f + part
    y_ref[0] = _layer_norm(alpha * x1 + (1.0 + gf_ref[0]) * f, ln2g_ref[...], ln2b_ref[...])


def _post(alpha, sample, tm, x, convout, attn, mods, weights, extra):
    nb, s, d = x.shape
    r = mods[0].shape[1]
    d_ff = weights[5].shape[1]
    n_ff = 2 if d_ff % (2 * LANES) == 0 else 1
    ff_chunks = tuple((i * d_ff // n_ff, (i + 1) * d_ff // n_ff) for i in range(n_ff))
    row = lambda b, i: (b, i, 0)
    mod_spec = pl.BlockSpec((1, r, d), lambda b, i: (b, 0, 0))
    in_specs = [pl.BlockSpec((1, tm, d), row),
                pl.BlockSpec((1, tm, convout.shape[2]), row),
                pl.BlockSpec((1, tm, attn.shape[2]), row),
                mod_spec, mod_spec, mod_spec, mod_spec]
    in_specs += [_const_spec(w.shape) for w in weights]
    in_specs += [_const_spec(e.shape) for e in extra]
    return pl.pallas_call(
        functools.partial(_post_body, alpha, ff_chunks, sample),
        grid=(nb, s // tm), in_specs=in_specs,
        out_specs=pl.BlockSpec((1, tm, d), row),
        out_shape=jax.ShapeDtypeStruct((nb, s, d), F32),
        compiler_params=pltpu.CompilerParams(dimension_semantics=("arbitrary", "arbitrary"),
                                             vmem_limit_bytes=VMEM_LIMIT),
        name="post_sample" if sample else "post_prompt",
    )(x, convout, attn, *mods, *weights, *extra)


def _rope_tables(pos, rope, nope):
    half = rope // 2
    inv = 1.0 / (ROPE_THETA ** (jnp.arange(0, rope, 2, dtype=F32) / rope))
    ang = pos.astype(F32)[:, None] * inv[None, :]
    cos, sin = jnp.cos(ang), jnp.sin(ang)
    n = pos.shape[0]
    z = lambda w: jnp.zeros((n, w), F32)
    pad = LANES - nope - rope
    ta = jnp.concatenate([jnp.ones((n, nope), F32), cos, cos, z(pad)], axis=1)
    tb = jnp.concatenate([z(nope + half), sin, z(pad)], axis=1)
    tc = jnp.concatenate([z(nope), -sin, z(half + pad)], axis=1)
    return ta, tb, tc


def kernel(x_prompt, x_sample, cache_latent, cache_k_rope, state_conv, page_table, c_prompt, c_sample,
           w_ada, b_ada, w_in, conv_w, g_q, g_kv, w_uq, w_uk, w_uv, w_o, ln1_g, ln1_b, w_gate, w_up,
           w_down, ln2_g, ln2_b):
    depth = w_ada.shape[0]
    nb, seq, d = x_prompt.shape
    ns, dec_seq, _ = x_sample.shape
    assert dec_seq == 1
    conv_dim = conv_w.shape[2]
    q_rank = g_q.shape[1]
    kv_rank = g_kv.shape[1]
    n_heads, nope = w_uk.shape[2], w_uk.shape[3]
    rope = w_uq.shape[3] - nope
    v_dim = w_uv.shape[3]
    half = rope // 2
    page = cache_latent.shape[2]
    past_len = page_table.shape[1] * page
    alpha = (2 * depth) ** 0.25
    scale = (nope + rope) ** -0.5 * LOG2E
    dims = (conv_dim, q_rank, kv_rank, n_heads, nope, half, scale)
    pad = HEAD_PAD - nope - rope
    assert pad >= 0 and 2 * v_dim == HEAD_PAD

    tabs_p = _rope_tables(jnp.arange(seq), rope, nope)
    tabs_s = _rope_tables(jnp.full((ns,), past_len), rope, nope)

    xp = x_prompt
    xs = x_sample.reshape(1, ns, d)
    c_all = jnp.concatenate([c_prompt, c_sample], axis=0)
    c_all = jnp.pad(c_all, ((0, -(nb + ns) % (2 * SUBLANES)), (0, 0)))
    outs = [[] for _ in range(6)]
    for l in range(depth):
        w_in_l = w_in[l]
        c5 = 3 * conv_dim + q_rank + kv_rank
        wmain = w_in_l[:, :c5].astype(BF16)
        wkr = jnp.pad(w_in_l[:, c5:], ((0, 0), (nope, pad))).astype(BF16)
        wq = jnp.pad(w_uq[l], ((0, 0), (0, 0), (0, pad))).reshape(q_rank, n_heads * HEAD_PAD).astype(BF16)
        wk = jnp.pad(w_uk[l], ((0, 0), (0, 0), (0, HEAD_PAD - nope))).reshape(kv_rank, n_heads * HEAD_PAD)
        wk = wk.astype(BF16)
        wv = w_uv[l].reshape(kv_rank, n_heads * v_dim).astype(BF16)
        wukt = jnp.transpose(w_uk[l], (1, 2, 0)).astype(BF16)
        eye = jnp.eye(n_heads, dtype=F32)
        wuv_bd = (w_uv[l].transpose(1, 0, 2)[:, :, None, :] * eye[:, None, :, None])
        wuv_bd = wuv_bd.reshape(n_heads * kv_rank, n_heads * v_dim).astype(BF16)
        woc = w_o[l, :conv_dim].astype(BF16)
        woa = w_o[l, conv_dim:].astype(BF16)
        proj_w = (wmain, wkr, conv_w[l], g_q[l].reshape(1, -1), g_kv[l].reshape(1, -1), wq)
        post_w = (woc, woa, ln1_g[l].reshape(1, -1), ln1_b[l].reshape(1, -1), w_gate[l].astype(BF16),
                  w_up[l].astype(BF16), w_down[l].astype(BF16), ln2_g[l].reshape(1, -1),
                  ln2_b[l].reshape(1, -1))

        mod = _ada(c_all, w_ada[l], b_ada[l])
        mp = [mod[:nb, i * d:(i + 1) * d].reshape(nb, 1, d) for i in range(6)]
        ms = [mod[nb:nb + ns, i * d:(i + 1) * d].reshape(1, ns, d) for i in range(6)]

        convout, q, k, v, lat_p, kr_p, tail = _proj(dims, False, 512, xp, mp[0], mp[1], tabs_p, proj_w,
                                                     (wk, wv))
        attn = _attn_prompt(q, k, v, n_heads)
        xp = _post(alpha, False, 512, xp, convout, attn, (mp[2], mp[3], mp[4], mp[5]), post_w, ())

        st = state_conv[l]
        convout_s, q_s, qlat_s, lat_s, kr_s, cin_s = _proj(
            dims, True, ns, xs, ms[0], ms[1], tabs_s, proj_w, (st[:, 0], st[:, 1], wukt))
        rows_pad = ((0, 0), (0, 2 * SUBLANES - n_heads), (0, 0))
        q16 = jnp.pad(q_s.reshape(ns, n_heads, HEAD_PAD), rows_pad)
        qlat16 = jnp.pad(qlat_s.reshape(ns, n_heads, kv_rank), rows_pad)
        cache_krt = jnp.swapaxes(cache_k_rope[l], 1, 2)
        olat = _attn_sample(page_table, q16, qlat16, lat_s[0], kr_s[0], cache_latent[l], cache_krt,
                            n_heads, nope)
        xs = _post(alpha, True, ns, xs, convout_s, olat.reshape(1, ns, n_heads * kv_rank),
                   (ms[2], ms[3], ms[4], ms[5]), post_w, (wuv_bd,))

        outs[0].append(lat_p)
        outs[1].append(kr_p)
        outs[2].append(tail[:, SUBLANES - 2:, :])
        outs[3].append(lat_s.reshape(ns, 1, kv_rank))
        outs[4].append(kr_s.reshape(ns, 1, rope))
        outs[5].append(jnp.stack([st[:, 1], cin_s[0]], axis=1))
    return (xp, xs.reshape(ns, 1, d), jnp.stack(outs[0]), jnp.stack(outs[1]), jnp.stack(outs[2]),
            jnp.stack(outs[3]), jnp.stack(outs[4]), jnp.stack(outs[5]))
```

```python
import functools
import math

import jax
import jax.numpy as jnp
from jax import lax
from jax.experimental import pallas as pl
from jax.experimental.pallas import tpu as pltpu

F32 = jnp.float32
BF16 = jnp.bfloat16

ROPE_THETA = 10000.0
LN_EPS = 1e-5
RMS_EPS = 1e-6
NEG_INF = -1e30
LOG2E = math.log2(math.e)

LANES = 128
SUBLANES = 8
MXU_TILE = 256
HEAD_PAD = 128
VMEM_LIMIT = 56 * 1024 * 1024


def _dot(a, b):
    return jnp.dot(a, b, preferred_element_type=F32)


def _dot_nt(a, b):
    return lax.dot_general(a, b, (((1,), (1,)), ((), ())), preferred_element_type=F32)


def _layer_norm(r, g, b):
    mu = jnp.mean(r, axis=-1, keepdims=True)
    d = r - mu
    var = jnp.mean(d * d, axis=-1, keepdims=True)
    return d * lax.rsqrt(var + LN_EPS) * g + b


def _rms_norm(x, g):
    return x * lax.rsqrt(jnp.mean(x * x, axis=-1, keepdims=True) + RMS_EPS) * g


def _rope_group(x, ta, tb, tc, half):
    return x * ta + pltpu.roll(x, half, 1) * tb + pltpu.roll(x, LANES - half, 1) * tc


def _ada_body(c_ref, w_ref, b_ref, o_ref):
    c = c_ref[...]
    s = c * jax.nn.sigmoid(c)
    o_ref[...] = _dot(s.astype(BF16), w_ref[...].astype(BF16)) + b_ref[...]


def _ada(c_all, w_ada, b_ada, tn=1024):
    m, d = c_all.shape
    n = w_ada.shape[1]
    return pl.pallas_call(
        _ada_body,
        grid=(n // tn,),
        in_specs=[pl.BlockSpec((m, d), lambda j: (0, 0)),
                  pl.BlockSpec((d, tn), lambda j: (0, j)),
                  pl.BlockSpec((1, tn), lambda j: (0, j))],
        out_specs=pl.BlockSpec((m, tn), lambda j: (0, j)),
        out_shape=jax.ShapeDtypeStruct((m, n), F32),
        compiler_params=pltpu.CompilerParams(dimension_semantics=("arbitrary",),
                                             vmem_limit_bytes=VMEM_LIMIT),
        name="ada",
    )(c_all, w_ada, b_ada.reshape(1, n))


def _proj_body(dims, sample, *refs):
    conv_dim, q_rank, kv_rank, n_heads, nope, half, scale = dims
    if sample:
        (x_ref, sh_ref, sc_ref, ta_ref, tb_ref, tc_ref, wmain_ref, convw_ref, gq_ref, gkv_ref,
         wq_ref, s0_ref, s1_ref, wukt_ref,
         convout_ref, q_ref, qlat_ref, lat_ref, krot_ref, cin_ref) = refs
    else:
        (x_ref, sh_ref, sc_ref, ta_ref, tb_ref, tc_ref, wmain_ref, convw_ref, gq_ref, gkv_ref,
         wq_ref, wk_ref, wv_ref,
         convout_ref, q_ref, k_ref, v_ref, lat_ref, krot_ref, tail_ref, cin_buf) = refs
    tm = x_ref.shape[1]
    c1, c2, c3 = conv_dim, 2 * conv_dim, 3 * conv_dim
    c4 = c3 + q_rank + LANES
    c5 = c4 + kv_rank

    x = x_ref[0]
    u = x * (1.0 + sc_ref[0]) + sh_ref[0]
    ub = u.astype(BF16)
    h = _dot(ub, wmain_ref[:, 0:c1])
    gb = _dot(ub, wmain_ref[:, c1:c2])
    gc = _dot(ub, wmain_ref[:, c2:c3])
    cq_kr = _dot(ub, wmain_ref[:, c3:c4])
    cq = cq_kr[:, 0:q_rank]
    kr = cq_kr[:, q_rank:q_rank + LANES]
    ckv = _dot(ub, wmain_ref[:, c4:c5])

    conv_in = gc * h
    w0 = convw_ref[0:1, :]
    w1 = convw_ref[1:2, :]
    w2 = convw_ref[2:3, :]
    if sample:
        y = w0 * s0_ref[...] + w1 * s1_ref[...] + w2 * conv_in
        cin_ref[0] = conv_in
    else:
        @pl.when(pl.program_id(1) == 0)
        def _():
            cin_buf[0:SUBLANES, :] = jnp.zeros((SUBLANES, conv_dim), F32)
        cin_buf[SUBLANES:SUBLANES + tm, :] = conv_in
        y = (w0 * cin_buf[SUBLANES - 2:SUBLANES - 2 + tm, :]
             + w1 * cin_buf[SUBLANES - 1:SUBLANES - 1 + tm, :] + w2 * conv_in)
        tail = conv_in[tm - SUBLANES:tm, :]
        cin_buf[0:SUBLANES, :] = tail
        tail_ref[0] = tail
    convout_ref[0] = (gb * y).astype(BF16)

    ta = ta_ref[...]
    tb = tb_ref[...]
    tc = tc_ref[...]
    cqn = _rms_norm(cq, gq_ref[...]).astype(BF16)
    qf = _dot(cqn, wq_ref[...])
    q_heads = [_rope_group(qf[:, HEAD_PAD * i:HEAD_PAD * (i + 1)], ta, tb, tc, half) * scale
               for i in range(n_heads)]
    q_ref[0] = jnp.concatenate(q_heads, axis=1).astype(q_ref.dtype)

    latent = _rms_norm(ckv, gkv_ref[...])
    lat_ref[0] = latent
    krot = _rope_group(kr, ta, tb, tc, half)
    krot_ref[0] = krot[:, nope:nope + 2 * half]

    if sample:
        for i in range(n_heads):
            qn = q_heads[i][:, 0:nope].astype(BF16)
            qlat_ref[0, :, kv_rank * i:kv_rank * (i + 1)] = _dot(qn, wukt_ref[i])
    else:
        latb = latent.astype(BF16)
        kf = _dot(latb, wk_ref[...])
        k_ref[0] = (kf + jnp.concatenate([krot] * n_heads, axis=1)).astype(BF16)
        v_ref[0] = _dot(latb, wv_ref[...]).astype(BF16)


def _const_spec(shape):
    nd = len(shape)
    return pl.BlockSpec(shape, lambda *_: (0,) * nd, pipeline_mode=pl.Buffered(1))


def _proj(dims, sample, tm, x, sh, sc, tabs, weights, extra):
    conv_dim, q_rank, kv_rank, n_heads, nope, half, _ = dims
    nb, s, d = x.shape
    r = sh.shape[1]
    grid = (nb, s // tm)
    row = lambda b, i: (b, i, 0)
    mod_spec = pl.BlockSpec((1, r, d), (lambda b, i: (b, 0, 0)))
    tab_spec = pl.BlockSpec((tm, LANES), lambda b, i: (i, 0))
    wmain, convw, gq, gkv, wq = weights
    in_specs = [pl.BlockSpec((1, tm, d), row), mod_spec, mod_spec, tab_spec, tab_spec, tab_spec,
                _const_spec(wmain.shape), _const_spec(convw.shape),
                _const_spec(gq.shape), _const_spec(gkv.shape), _const_spec(wq.shape)]
    in_specs += [_const_spec(e.shape) for e in extra]
    hq = n_heads * HEAD_PAD
    out_shape = [jax.ShapeDtypeStruct((nb, s, conv_dim), BF16)]
    out_specs = [pl.BlockSpec((1, tm, conv_dim), row)]
    if sample:
        out_shape += [jax.ShapeDtypeStruct((nb, s, hq), F32),
                      jax.ShapeDtypeStruct((nb, s, n_heads * kv_rank), F32)]
        out_specs += [pl.BlockSpec((1, tm, hq), row), pl.BlockSpec((1, tm, n_heads * kv_rank), row)]
    else:
        out_shape += [jax.ShapeDtypeStruct((nb, s, hq), BF16), jax.ShapeDtypeStruct((nb, s, hq), BF16),
                      jax.ShapeDtypeStruct((nb, s, n_heads * (hq // n_heads // 2)), BF16)]
        out_specs += [pl.BlockSpec((1, tm, hq), row), pl.BlockSpec((1, tm, hq), row),
                      pl.BlockSpec((1, tm, n_heads * (hq // n_heads // 2)), row)]
    out_shape += [jax.ShapeDtypeStruct((nb, s, kv_rank), F32), jax.ShapeDtypeStruct((nb, s, 2 * half), F32)]
    out_specs += [pl.BlockSpec((1, tm, kv_rank), row), pl.BlockSpec((1, tm, 2 * half), row)]
    scratch = []
    if sample:
        out_shape += [jax.ShapeDtypeStruct((nb, s, conv_dim), F32)]
        out_specs += [pl.BlockSpec((1, tm, conv_dim), row)]
    else:
        out_shape += [jax.ShapeDtypeStruct((nb, SUBLANES, conv_dim), F32)]
        out_specs += [pl.BlockSpec((1, SUBLANES, conv_dim), lambda b, i: (b, 0, 0))]
        scratch = [pltpu.VMEM((SUBLANES + tm, conv_dim), F32)]
    return pl.pallas_call(
        functools.partial(_proj_body, dims, sample),
        grid=grid, in_specs=in_specs, out_specs=out_specs, out_shape=out_shape,
        scratch_shapes=scratch,
        compiler_params=pltpu.CompilerParams(dimension_semantics=("arbitrary", "arbitrary"),
                                             vmem_limit_bytes=VMEM_LIMIT),
        name="proj_sample" if sample else "proj_prompt",
    )(x, sh, sc, *tabs, *weights, *extra)


def _attn_prompt_body(v_dim, q_ref, k_ref, v_ref, o_ref, vt_ref):
    tq = q_ref.shape[1]
    tk = tq
    qi = pl.program_id(2)

    @pl.when(qi == 0)
    def _():
        vt_ref[...] = v_ref[0].astype(F32).T.astype(BF16)

    ts = tk
    keys = lax.broadcasted_iota(jnp.int32, (ts, tq), 0)
    queries = lax.broadcasted_iota(jnp.int32, (ts, tq), 1)
    n_h = q_ref.shape[2] // HEAD_PAD
    qts = [q_ref[0, :, HEAD_PAD * hh:HEAD_PAD * (hh + 1)].astype(F32).T.astype(BF16) for hh in range(n_h)]

    def step(j, carry, masked):
        new = list(carry)
        for r in range(tk // ts):
            at = pl.ds(pl.multiple_of(j * tk + r * ts, ts), ts)
            for hh, (m, l, acc) in enumerate(new):
                k = k_ref[0, at, HEAD_PAD * hh:HEAD_PAD * (hh + 1)]
                vt = vt_ref[v_dim * hh:v_dim * (hh + 1), at]
                s = _dot(k, qts[hh])
                if masked:
                    s = jnp.where(keys + r * ts <= queries, s, NEG_INF)
                m_new = jnp.maximum(m, jnp.max(s, axis=0, keepdims=True))
                p = jnp.exp2(s - m_new)
                alpha = jnp.exp2(m - m_new)
                l = alpha * l + jnp.sum(p, axis=0, keepdims=True)
                acc = alpha * acc + _dot(vt, p.astype(BF16))
                new[hh] = (m_new, l, acc)
        return tuple(new)

    init = tuple((jnp.full((1, tq), NEG_INF, F32), jnp.zeros((1, tq), F32), jnp.zeros((v_dim, tq), F32))
                 for _ in range(n_h))
    carry = lax.fori_loop(0, qi, functools.partial(step, masked=False), init)
    carry = step(qi, carry, True)
    ot = jnp.concatenate([acc / l for _, l, acc in carry], axis=0)
    o_ref[0] = ot.T.astype(o_ref.dtype)


def _attn_prompt(q, k, v, n_heads, tq=512, hg=4):
    b, s, _ = q.shape
    v_dim = v.shape[2] // n_heads
    assert n_heads % hg == 0 and (hg * v_dim) % LANES == 0
    return pl.pallas_call(
        functools.partial(_attn_prompt_body, v_dim),
        grid=(b, n_heads // hg, s // tq),
        in_specs=[pl.BlockSpec((1, tq, hg * HEAD_PAD), lambda bi, hp, qi: (bi, qi, hp)),
                  pl.BlockSpec((1, s, hg * HEAD_PAD), lambda bi, hp, qi: (bi, 0, hp)),
                  pl.BlockSpec((1, s, hg * v_dim), lambda bi, hp, qi: (bi, 0, hp))],
        out_specs=pl.BlockSpec((1, tq, hg * v_dim), lambda bi, hp, qi: (bi, qi, hp)),
        out_shape=jax.ShapeDtypeStruct((b, s, n_heads * v_dim), BF16),
        scratch_shapes=[pltpu.VMEM((hg * v_dim, s), BF16)],
        compiler_params=pltpu.CompilerParams(dimension_semantics=("arbitrary",) * 3,
                                             vmem_limit_bytes=VMEM_LIMIT),
        name="attn_prompt",
    )(q, k, v)


def _attn_sample_body(geom, pt_ref, q_ref, qlat_ref, latn_ref, krn_ref, lat_hbm, krt_hbm, o_ref,
                      lat_buf, kr_buf, kb_buf, s_buf, st_ref, acc_ref, p_ref, sems):
    nseq, n_pages, ch, page, nope, rope = geom
    n_chunks = n_pages // ch
    n_steps = nseq * n_chunks
    n_heads = o_ref.shape[1]
    hp = q_ref.shape[1]
    n_slots = lat_buf.shape[0]
    n_kb = kb_buf.shape[0]

    def page_copies(pg, slot, j):
        cols = pl.ds(j * page, page)
        return (pltpu.make_async_copy(lat_hbm.at[pg], lat_buf.at[slot, cols], sems.at[0, slot]),
                pltpu.make_async_copy(krt_hbm.at[pg], kr_buf.at[slot, :, cols], sems.at[1, slot]))

    def start_chunk(g, slot):
        for j in range(ch):
            for cp in page_copies(pt_ref[g * ch + j], slot, j):
                cp.start()

    def wait_chunk(slot):
        pltpu.make_async_copy(lat_buf.at[slot], lat_buf.at[slot], sems.at[0, slot]).wait()
        pltpu.make_async_copy(kr_buf.at[slot], kr_buf.at[slot], sems.at[1, slot]).wait()

    def put(i, col):
        st_ref[i] = jnp.broadcast_to(col, st_ref.shape[1:])

    def substep(a, u):
        m, l, alpha = st_ref[0][:, 0:1], st_ref[1][:, 0:1], st_ref[2][:, 0:1]
        acc, p = acc_ref[...], p_ref[...]
        wait_chunk(u)

        acc = alpha * acc + _dot(p, kb_buf[(u - 2) % n_kb])
        out_row = jnp.minimum(lax.div(jnp.maximum(a - 2, 0), n_chunks), nseq)
        o_ref[out_row] = (acc / l)[0:n_heads, :]

        seq_a = jnp.minimum(lax.div(a, n_chunks), nseq - 1)
        refill = jnp.minimum(a + n_slots - 1, n_steps - 1) * ch
        rslot = (u + n_slots - 1) % n_slots
        pages = []
        for j in range(ch):
            rows = pl.ds(j * page, page)
            kbj = lat_buf[u, rows, :].astype(BF16)
            kb_buf[u % n_kb, rows, :] = kbj
            pages.append(kbj)
            for cp in page_copies(pt_ref[refill + j], rslot, j):
                cp.start()
        kb = jnp.concatenate(pages, axis=0)
        krb = kr_buf[u].astype(BF16)
        qa = q_ref[seq_a]
        s_buf[u % 2] = (_dot_nt(qlat_ref[seq_a].astype(BF16), kb)
                        + _dot(qa[:, nope:nope + rope].astype(BF16), krb))

        b = jnp.maximum(a - 1, 0)
        seq_b = jnp.minimum(lax.div(b, n_chunks), nseq - 1)
        first = lax.rem(b, n_chunks) == 0
        qb = q_ref[seq_b]
        qlat = qlat_ref[seq_b]
        latn = latn_ref[pl.ds(seq_b, 1), :]
        krn = krn_ref[pl.ds(seq_b, 1), :]
        s_new = (jnp.sum(qlat * latn, axis=-1, keepdims=True)
                 + jnp.sum(qb[:, nope:nope + rope] * krn, axis=-1, keepdims=True))
        m = jnp.where(first, s_new, m)
        l = jnp.where(first, 1.0, l)
        acc = jnp.where(first, jnp.broadcast_to(latn, acc.shape), acc)
        s = s_buf[(u - 1) % 2]
        m_new = jnp.maximum(m, jnp.max(s, axis=-1, keepdims=True))
        p = jnp.exp2(s - m_new)
        alpha = jnp.exp2(m - m_new)
        put(0, m_new)
        put(1, alpha * l + jnp.sum(p, axis=-1, keepdims=True))
        put(2, alpha)
        acc_ref[...] = acc
        p_ref[...] = p.astype(BF16)

    i = pl.program_id(0)

    @pl.when(i == 0)
    def _():
        for c in range(n_slots - 1):
            start_chunk(i + c, c)
        for k in range(2, n_kb):
            kb_buf[k] = jnp.zeros(kb_buf.shape[1:], BF16)
        s_buf[1] = jnp.zeros(s_buf.shape[1:], F32)
        p_ref[...] = jnp.zeros(p_ref.shape, BF16)
        acc_ref[...] = jnp.zeros(acc_ref.shape, F32)
        put(0, jnp.zeros((hp, 1), F32))
        put(1, jnp.ones((hp, 1), F32))
        put(2, jnp.zeros((hp, 1), F32))

    for u in range(n_slots):
        pl.when(i >= 0)(functools.partial(substep, n_slots * i + u, u))

    @pl.when(i == pl.num_programs(0) - 1)
    def _():
        for slot in range(n_slots - 1):
            wait_chunk(slot)


def _attn_sample(page_table, q16, qlat16, lat_new, kr_new, cache_lat, cache_krt, n_heads, nope,
                 ch=32, n_slots=4):
    nseq, n_pages = page_table.shape
    _, page, kv_rank = cache_lat.shape
    rope = cache_krt.shape[1]
    assert n_pages % ch == 0 and n_slots % 4 == 0
    chk = ch * page
    hp = q16.shape[1]
    geom = (nseq, n_pages, ch, page, nope, rope)
    positions = nseq * (n_pages // ch) + 2
    whole = lambda a: pl.BlockSpec(a.shape, lambda i, pt: (0,) * a.ndim)
    grid_spec = pltpu.PrefetchScalarGridSpec(
        num_scalar_prefetch=1,
        grid=(pl.cdiv(positions, n_slots),),
        in_specs=[whole(q16), whole(qlat16), whole(lat_new), whole(kr_new),
                  pl.BlockSpec(memory_space=pl.ANY), pl.BlockSpec(memory_space=pl.ANY)],
        out_specs=pl.BlockSpec((nseq + 1, n_heads, kv_rank), lambda i, pt: (0, 0, 0)),
        scratch_shapes=[pltpu.VMEM((n_slots, chk, kv_rank), F32),
                        pltpu.VMEM((n_slots, rope, chk), F32),
                        pltpu.VMEM((4, chk, kv_rank), BF16),
                        pltpu.VMEM((2, hp, chk), F32),
                        pltpu.VMEM((3, hp, LANES), F32),
                        pltpu.VMEM((hp, kv_rank), F32),
                        pltpu.VMEM((hp, chk), BF16),
                        pltpu.SemaphoreType.DMA((2, n_slots))],
    )
    out = pl.pallas_call(
        functools.partial(_attn_sample_body, geom),
        grid_spec=grid_spec,
        out_shape=jax.ShapeDtypeStruct((nseq + 1, n_heads, kv_rank), F32),
        compiler_params=pltpu.CompilerParams(dimension_semantics=("arbitrary",),
                                             vmem_limit_bytes=VMEM_LIMIT),
        name="attn_sample",
    )(page_table.reshape(-1), q16, qlat16, lat_new, kr_new, cache_lat, cache_krt)
    return out[:nseq]


def _post_body(alpha, ff_chunks, sample, *refs):
    if sample:
        (x_ref, convout_ref, olat_ref, ga_ref, shf_ref, scf_ref, gf_ref, woc_ref, woa_ref,
         ln1g_ref, ln1b_ref, wg_ref, wu_ref, wd_ref, ln2g_ref, ln2b_ref, wuv_ref, y_ref) = refs
        attn = _dot(olat_ref[0].astype(BF16), wuv_ref[...]).astype(BF16)
    else:
        (x_ref, convout_ref, attn_ref, ga_ref, shf_ref, scf_ref, gf_ref, woc_ref, woa_ref,
         ln1g_ref, ln1b_ref, wg_ref, wu_ref, wd_ref, ln2g_ref, ln2b_ref, y_ref) = refs
        attn = attn_ref[0]
    x = x_ref[0]
    a = _dot(convout_ref[0], woc_ref[...]) + _dot(attn, woa_ref[...])
    x1 = _layer_norm(alpha * x + (1.0 + ga_ref[0]) * a, ln1g_ref[...], ln1b_ref[...])
    ub = (x1 * (1.0 + scf_ref[0]) + shf_ref[0]).astype(BF16)
    f = None
    for lo, hi in ff_chunks:
        g = _dot(ub, wg_ref[:, lo:hi])
        up = _dot(ub, wu_ref[:, lo:hi])
        hmid = (g * jax.nn.sigmoid(g) * up).astype(BF16)
        part = _dot(hmid, wd_ref[lo:hi, :])
        f = part if f is None else f + part
    y_ref[0] = _layer_norm(alpha * x1 + (1.0 + gf_ref[0]) * f, ln2g_ref[...], ln2b_ref[...])


def _post(alpha, sample, tm, x, convout, attn, mods, weights, extra):
    nb, s, d = x.shape
    r = mods[0].shape[1]
    d_ff = weights[5].shape[1]
    cut = min(d_ff, pl.cdiv(d_ff // 2, MXU_TILE) * MXU_TILE)
    ff_chunks = tuple(c for c in ((0, cut), (cut, d_ff)) if c[1] > c[0])
    row = lambda b, i: (b, i, 0)
    mod_spec = pl.BlockSpec((1, r, d), lambda b, i: (b, 0, 0))
    in_specs = [pl.BlockSpec((1, tm, d), row),
                pl.BlockSpec((1, tm, convout.shape[2]), row),
                pl.BlockSpec((1, tm, attn.shape[2]), row),
                mod_spec, mod_spec, mod_spec, mod_spec]
    in_specs += [_const_spec(w.shape) for w in weights]
    in_specs += [_const_spec(e.shape) for e in extra]
    return pl.pallas_call(
        functools.partial(_post_body, alpha, ff_chunks, sample),
        grid=(nb, s // tm), in_specs=in_specs,
        out_specs=pl.BlockSpec((1, tm, d), row),
        out_shape=jax.ShapeDtypeStruct((nb, s, d), F32),
        compiler_params=pltpu.CompilerParams(dimension_semantics=("arbitrary", "arbitrary"),
                                             vmem_limit_bytes=VMEM_LIMIT),
        name="post_sample" if sample else "post_prompt",
    )(x, convout, attn, *mods, *weights, *extra)


def _rope_tables(pos, rope, nope):
    half = rope // 2
    inv = 1.0 / (ROPE_THETA ** (jnp.arange(0, rope, 2, dtype=F32) / rope))
    ang = pos.astype(F32)[:, None] * inv[None, :]
    cos, sin = jnp.cos(ang), jnp.sin(ang)
    n = pos.shape[0]
    z = lambda w: jnp.zeros((n, w), F32)
    pad = LANES - nope - rope
    ta = jnp.concatenate([jnp.ones((n, nope), F32), cos, cos, z(pad)], axis=1)
    tb = jnp.concatenate([z(nope + half), sin, z(pad)], axis=1)
    tc = jnp.concatenate([z(nope), -sin, z(half + pad)], axis=1)
    return ta, tb, tc


def kernel(x_prompt, x_sample, cache_latent, cache_k_rope, state_conv, page_table, c_prompt, c_sample,
           w_ada, b_ada, w_in, conv_w, g_q, g_kv, w_uq, w_uk, w_uv, w_o, ln1_g, ln1_b, w_gate, w_up,
           w_down, ln2_g, ln2_b):
    depth = w_ada.shape[0]
    nb, seq, d = x_prompt.shape
    ns, dec_seq, _ = x_sample.shape
    assert dec_seq == 1
    conv_dim = conv_w.shape[2]
    q_rank = g_q.shape[1]
    kv_rank = g_kv.shape[1]
    n_heads, nope = w_uk.shape[2], w_uk.shape[3]
    rope = w_uq.shape[3] - nope
    v_dim = w_uv.shape[3]
    half = rope // 2
    page = cache_latent.shape[2]
    past_len = page_table.shape[1] * page
    alpha = (2 * depth) ** 0.25
    scale = (nope + rope) ** -0.5 * LOG2E
    dims = (conv_dim, q_rank, kv_rank, n_heads, nope, half, scale)
    pad = HEAD_PAD - nope - rope
    assert pad >= 0 and 2 * v_dim == HEAD_PAD

    tabs_p = _rope_tables(jnp.arange(seq), rope, nope)
    tabs_s = _rope_tables(jnp.full((ns,), past_len), rope, nope)

    xp = x_prompt
    xs = x_sample.reshape(1, ns, d)
    c_all = jnp.concatenate([c_prompt, c_sample], axis=0)
    c_all = jnp.pad(c_all, ((0, -(nb + ns) % (2 * SUBLANES)), (0, 0)))
    outs = [[] for _ in range(6)]
    for l in range(depth):
        w_in_l = w_in[l]
        c4 = 3 * conv_dim + q_rank
        c5 = c4 + kv_rank
        wkr = jnp.pad(w_in_l[:, c5:], ((0, 0), (nope, pad)))
        wmain = jnp.concatenate([w_in_l[:, :c4], wkr, w_in_l[:, c4:c5]], axis=1).astype(BF16)
        wq = jnp.pad(w_uq[l], ((0, 0), (0, 0), (0, pad))).reshape(q_rank, n_heads * HEAD_PAD).astype(BF16)
        wk = jnp.pad(w_uk[l], ((0, 0), (0, 0), (0, HEAD_PAD - nope))).reshape(kv_rank, n_heads * HEAD_PAD)
        wk = wk.astype(BF16)
        wv = w_uv[l].reshape(kv_rank, n_heads * v_dim).astype(BF16)
        wukt = jnp.transpose(w_uk[l], (1, 2, 0)).astype(BF16)
        eye = jnp.eye(n_heads, dtype=F32)
        wuv_bd = (w_uv[l].transpose(1, 0, 2)[:, :, None, :] * eye[:, None, :, None])
        wuv_bd = wuv_bd.reshape(n_heads * kv_rank, n_heads * v_dim).astype(BF16)
        woc = w_o[l, :conv_dim].astype(BF16)
        woa = w_o[l, conv_dim:].astype(BF16)
        proj_w = (wmain, conv_w[l], g_q[l].reshape(1, -1), g_kv[l].reshape(1, -1), wq)
        post_w = (woc, woa, ln1_g[l].reshape(1, -1), ln1_b[l].reshape(1, -1), w_gate[l].astype(BF16),
                  w_up[l].astype(BF16), w_down[l].astype(BF16), ln2_g[l].reshape(1, -1),
                  ln2_b[l].reshape(1, -1))

        mod = _ada(c_all, w_ada[l], b_ada[l])
        mp = [mod[:nb, i * d:(i + 1) * d].reshape(nb, 1, d) for i in range(6)]
        ms = [mod[nb:nb + ns, i * d:(i + 1) * d].reshape(1, ns, d) for i in range(6)]

        convout, q, k, v, lat_p, kr_p, tail = _proj(dims, False, 512, xp, mp[0], mp[1], tabs_p, proj_w,
                                                     (wk, wv))
        attn = _attn_prompt(q, k, v, n_heads)
        xp = _post(alpha, False, 512, xp, convout, attn, (mp[2], mp[3], mp[4], mp[5]), post_w, ())

        st = state_conv[l]
        convout_s, q_s, qlat_s, lat_s, kr_s, cin_s = _proj(
            dims, True, ns, xs, ms[0], ms[1], tabs_s, proj_w, (st[:, 0], st[:, 1], wukt))
        rows_pad = ((0, 0), (0, 2 * SUBLANES - n_heads), (0, 0))
        q16 = jnp.pad(q_s.reshape(ns, n_heads, HEAD_PAD), rows_pad)
        qlat16 = jnp.pad(qlat_s.reshape(ns, n_heads, kv_rank), rows_pad)
        cache_krt = jnp.swapaxes(cache_k_rope[l], 1, 2)
        olat = _attn_sample(page_table, q16, qlat16, lat_s[0], kr_s[0], cache_latent[l], cache_krt,
                            n_heads, nope)
        xs = _post(alpha, True, ns, xs, convout_s, olat.reshape(1, ns, n_heads * kv_rank),
                   (ms[2], ms[3], ms[4], ms[5]), post_w, (wuv_bd,))

        outs[0].append(lat_p)
        outs[1].append(kr_p)
        outs[2].append(tail[:, SUBLANES - 2:, :])
        outs[3].append(lat_s.reshape(ns, 1, kv_rank))
        outs[4].append(kr_s.reshape(ns, 1, rope))
        outs[5].append(jnp.stack([st[:, 1], cin_s[0]], axis=1))
    return (xp, xs.reshape(ns, 1, d), jnp.stack(outs[0]), jnp.stack(outs[1]), jnp.stack(outs[2]),
            jnp.stack(outs[3]), jnp.stack(outs[4]), jnp.stack(outs[5]))
```

```python
import functools
import math

import jax
import jax.numpy as jnp
from jax import lax
from jax.experimental import pallas as pl
from jax.experimental.pallas import tpu as pltpu

F32 = jnp.float32
BF16 = jnp.bfloat16

ROPE_THETA = 10000.0
LN_EPS = 1e-5
RMS_EPS = 1e-6
NEG_INF = -1e30
LOG2E = math.log2(math.e)

LANES = 128
SUBLANES = 8
MXU_TILE = 256
PROJ_ROWS = 512
POST_ROWS = 512
HEAD_PAD = 128
VMEM_LIMIT = 56 * 1024 * 1024


def _dot(a, b):
    return jnp.dot(a, b, preferred_element_type=F32)


def _dot_nt(a, b):
    return lax.dot_general(a, b, (((1,), (1,)), ((), ())), preferred_element_type=F32)


def _layer_norm(r, g, b):
    mu = jnp.mean(r, axis=-1, keepdims=True)
    d = r - mu
    var = jnp.mean(d * d, axis=-1, keepdims=True)
    return d * lax.rsqrt(var + LN_EPS) * g + b


def _rms_norm(x, g):
    return x * lax.rsqrt(jnp.mean(x * x, axis=-1, keepdims=True) + RMS_EPS) * g


def _rope_group(x, ta, tb, tc, half):
    return x * ta + pltpu.roll(x, half, 1) * tb + pltpu.roll(x, LANES - half, 1) * tc


def _ada_body(c_ref, w_ref, b_ref, o_ref):
    c = c_ref[...]
    s = c * jax.nn.sigmoid(c)
    o_ref[...] = _dot(s.astype(BF16), w_ref[...].astype(BF16)) + b_ref[...]


def _ada(c_all, w_ada, b_ada, tn=1024):
    m, d = c_all.shape
    n = w_ada.shape[1]
    return pl.pallas_call(
        _ada_body,
        grid=(n // tn,),
        in_specs=[pl.BlockSpec((m, d), lambda j: (0, 0)),
                  pl.BlockSpec((d, tn), lambda j: (0, j)),
                  pl.BlockSpec((1, tn), lambda j: (0, j))],
        out_specs=pl.BlockSpec((m, tn), lambda j: (0, j)),
        out_shape=jax.ShapeDtypeStruct((m, n), F32),
        compiler_params=pltpu.CompilerParams(dimension_semantics=("arbitrary",),
                                             vmem_limit_bytes=VMEM_LIMIT),
        name="ada",
    )(c_all, w_ada, b_ada.reshape(1, n))


def _proj_body(dims, sample, *refs):
    conv_dim, q_rank, kv_rank, n_heads, nope, half, scale = dims
    if sample:
        (x_ref, sh_ref, sc_ref, ta_ref, tb_ref, tc_ref, wmain_ref, convw_ref, gq_ref, gkv_ref,
         wq_ref, s0_ref, s1_ref, wukt_ref,
         convout_ref, q_ref, qlat_ref, lat_ref, krot_ref, cin_ref) = refs
    else:
        (x_ref, sh_ref, sc_ref, ta_ref, tb_ref, tc_ref, wmain_ref, convw_ref, gq_ref, gkv_ref,
         wq_ref, wk_ref, wv_ref,
         convout_ref, q_ref, k_ref, v_ref, lat_ref, krot_ref, tail_ref, cin_buf) = refs
    tm = x_ref.shape[1]
    c1, c2, c3 = conv_dim, 2 * conv_dim, 3 * conv_dim
    c4 = c3 + q_rank + LANES
    c5 = c4 + kv_rank
    w0 = convw_ref[0:1, :]
    w1 = convw_ref[1:2, :]
    w2 = convw_ref[2:3, :]
    if not sample:
        @pl.when(pl.program_id(1) == 0)
        def _():
            cin_buf[0:SUBLANES, :] = jnp.zeros((SUBLANES, conv_dim), F32)

    def chunk(rows, sub):
        x = x_ref[0, rows, :]
        sc = sc_ref[0] if sc_ref.shape[1] == 1 else sc_ref[0, rows, :]
        sh = sh_ref[0] if sh_ref.shape[1] == 1 else sh_ref[0, rows, :]
        ub = (x * (1.0 + sc) + sh).astype(BF16)
        h = _dot(ub, wmain_ref[:, 0:c1])
        gb = _dot(ub, wmain_ref[:, c1:c2])
        gc = _dot(ub, wmain_ref[:, c2:c3])
        cq_kr = _dot(ub, wmain_ref[:, c3:c4])
        cq = cq_kr[:, 0:q_rank]
        kr = cq_kr[:, q_rank:q_rank + LANES]
        ckv = _dot(ub, wmain_ref[:, c4:c5])

        conv_in = gc * h
        if sample:
            y = w0 * s0_ref[rows, :] + w1 * s1_ref[rows, :] + w2 * conv_in
            cin_ref[0, rows, :] = conv_in
        else:
            cin_buf[SUBLANES:SUBLANES + sub, :] = conv_in
            y = (w0 * cin_buf[SUBLANES - 2:SUBLANES - 2 + sub, :]
                 + w1 * cin_buf[SUBLANES - 1:SUBLANES - 1 + sub, :] + w2 * conv_in)
            tail = conv_in[sub - SUBLANES:sub, :]
            cin_buf[0:SUBLANES, :] = tail
            tail_ref[0] = tail
        convout_ref[0, rows, :] = (gb * y).astype(BF16)

        ta = ta_ref[rows, :]
        tb = tb_ref[rows, :]
        tc = tc_ref[rows, :]
        cqn = _rms_norm(cq, gq_ref[...]).astype(BF16)
        qf = _dot(cqn, wq_ref[...])
        q_heads = [_rope_group(qf[:, HEAD_PAD * i:HEAD_PAD * (i + 1)], ta, tb, tc, half) * scale
                   for i in range(n_heads)]
        q_ref[0, rows, :] = jnp.concatenate(q_heads, axis=1).astype(q_ref.dtype)

        latent = _rms_norm(ckv, gkv_ref[...])
        lat_ref[0, rows, :] = latent
        krot = _rope_group(kr, ta, tb, tc, half)
        krot_ref[0, rows, :] = krot[:, nope:nope + 2 * half]

        if sample:
            for i in range(n_heads):
                qn = q_heads[i][:, 0:nope].astype(BF16)
                qlat_ref[0, rows, kv_rank * i:kv_rank * (i + 1)] = _dot(qn, wukt_ref[i])
        else:
            latb = latent.astype(BF16)
            kf = _dot(latb, wk_ref[...])
            k_ref[0, rows, :] = (kf + jnp.concatenate([krot] * n_heads, axis=1)).astype(BF16)
            v_ref[0, rows, :] = _dot(latb, wv_ref[...]).astype(BF16)

    sub = min(tm, PROJ_ROWS)
    for r0 in range(0, tm, sub):
        chunk(pl.ds(r0, sub), sub)


def _const_spec(shape):
    nd = len(shape)
    return pl.BlockSpec(shape, lambda *_: (0,) * nd, pipeline_mode=pl.Buffered(1))


def _proj(dims, sample, tm, x, sh, sc, tabs, weights, extra):
    conv_dim, q_rank, kv_rank, n_heads, nope, half, _ = dims
    nb, s, d = x.shape
    r = sh.shape[1]
    grid = (nb, s // tm)
    row = lambda b, i: (b, i, 0)
    mod_spec = pl.BlockSpec((1, r, d), (lambda b, i: (b, 0, 0)))
    tab_spec = pl.BlockSpec((tm, LANES), lambda b, i: (i, 0))
    wmain, convw, gq, gkv, wq = weights
    in_specs = [pl.BlockSpec((1, tm, d), row), mod_spec, mod_spec, tab_spec, tab_spec, tab_spec,
                _const_spec(wmain.shape), _const_spec(convw.shape),
                _const_spec(gq.shape), _const_spec(gkv.shape), _const_spec(wq.shape)]
    in_specs += [_const_spec(e.shape) for e in extra]
    hq = n_heads * HEAD_PAD
    out_shape = [jax.ShapeDtypeStruct((nb, s, conv_dim), BF16)]
    out_specs = [pl.BlockSpec((1, tm, conv_dim), row)]
    if sample:
        out_shape += [jax.ShapeDtypeStruct((nb, s, hq), F32),
                      jax.ShapeDtypeStruct((nb, s, n_heads * kv_rank), F32)]
        out_specs += [pl.BlockSpec((1, tm, hq), row), pl.BlockSpec((1, tm, n_heads * kv_rank), row)]
    else:
        out_shape += [jax.ShapeDtypeStruct((nb, s, hq), BF16), jax.ShapeDtypeStruct((nb, s, hq), BF16),
                      jax.ShapeDtypeStruct((nb, s, n_heads * (hq // n_heads // 2)), BF16)]
        out_specs += [pl.BlockSpec((1, tm, hq), row), pl.BlockSpec((1, tm, hq), row),
                      pl.BlockSpec((1, tm, n_heads * (hq // n_heads // 2)), row)]
    out_shape += [jax.ShapeDtypeStruct((nb, s, kv_rank), F32), jax.ShapeDtypeStruct((nb, s, 2 * half), F32)]
    out_specs += [pl.BlockSpec((1, tm, kv_rank), row), pl.BlockSpec((1, tm, 2 * half), row)]
    scratch = []
    if sample:
        out_shape += [jax.ShapeDtypeStruct((nb, s, conv_dim), F32)]
        out_specs += [pl.BlockSpec((1, tm, conv_dim), row)]
    else:
        out_shape += [jax.ShapeDtypeStruct((nb, SUBLANES, conv_dim), F32)]
        out_specs += [pl.BlockSpec((1, SUBLANES, conv_dim), lambda b, i: (b, 0, 0))]
        scratch = [pltpu.VMEM((SUBLANES + tm, conv_dim), F32)]
    return pl.pallas_call(
        functools.partial(_proj_body, dims, sample),
        grid=grid, in_specs=in_specs, out_specs=out_specs, out_shape=out_shape,
        scratch_shapes=scratch,
        compiler_params=pltpu.CompilerParams(dimension_semantics=("arbitrary", "arbitrary"),
                                             vmem_limit_bytes=VMEM_LIMIT),
        name="proj_sample" if sample else "proj_prompt",
    )(x, sh, sc, *tabs, *weights, *extra)


def _attn_prompt_body(v_dim, q_ref, k_ref, v_ref, o_ref, vt_ref):
    tq = q_ref.shape[1]
    tk = tq
    qi = pl.program_id(2)

    @pl.when(qi == 0)
    def _():
        vt_ref[...] = v_ref[0].astype(F32).T.astype(BF16)

    ts = tk
    keys = lax.broadcasted_iota(jnp.int32, (ts, tq), 0)
    queries = lax.broadcasted_iota(jnp.int32, (ts, tq), 1)
    n_h = q_ref.shape[2] // HEAD_PAD
    qts = [q_ref[0, :, HEAD_PAD * hh:HEAD_PAD * (hh + 1)].astype(F32).T.astype(BF16) for hh in range(n_h)]

    def step(j, carry, masked):
        new = list(carry)
        for r in range(tk // ts):
            at = pl.ds(pl.multiple_of(j * tk + r * ts, ts), ts)
            for hh, (m, l, acc) in enumerate(new):
                k = k_ref[0, at, HEAD_PAD * hh:HEAD_PAD * (hh + 1)]
                vt = vt_ref[v_dim * hh:v_dim * (hh + 1), at]
                s = _dot(k, qts[hh])
                if masked:
                    s = jnp.where(keys + r * ts <= queries, s, NEG_INF)
                m_new = jnp.maximum(m, jnp.max(s, axis=0, keepdims=True))
                p = jnp.exp2(s - m_new)
                alpha = jnp.exp2(m - m_new)
                l = alpha * l + jnp.sum(p, axis=0, keepdims=True)
                acc = alpha * acc + _dot(vt, p.astype(BF16))
                new[hh] = (m_new, l, acc)
        return tuple(new)

    init = tuple((jnp.full((1, tq), NEG_INF, F32), jnp.zeros((1, tq), F32), jnp.zeros((v_dim, tq), F32))
                 for _ in range(n_h))
    carry = lax.fori_loop(0, qi, functools.partial(step, masked=False), init)
    carry = step(qi, carry, True)
    ot = jnp.concatenate([acc / l for _, l, acc in carry], axis=0)
    o_ref[0] = ot.T.astype(o_ref.dtype)


def _attn_prompt(q, k, v, n_heads, tq=512, hg=8):
    b, s, _ = q.shape
    v_dim = v.shape[2] // n_heads
    assert n_heads % hg == 0 and (hg * v_dim) % LANES == 0
    return pl.pallas_call(
        functools.partial(_attn_prompt_body, v_dim),
        grid=(b, n_heads // hg, s // tq),
        in_specs=[pl.BlockSpec((1, tq, hg * HEAD_PAD), lambda bi, hp, qi: (bi, qi, hp)),
                  pl.BlockSpec((1, s, hg * HEAD_PAD), lambda bi, hp, qi: (bi, 0, hp)),
                  pl.BlockSpec((1, s, hg * v_dim), lambda bi, hp, qi: (bi, 0, hp))],
        out_specs=pl.BlockSpec((1, tq, hg * v_dim), lambda bi, hp, qi: (bi, qi, hp)),
        out_shape=jax.ShapeDtypeStruct((b, s, n_heads * v_dim), BF16),
        scratch_shapes=[pltpu.VMEM((hg * v_dim, s), BF16)],
        compiler_params=pltpu.CompilerParams(dimension_semantics=("arbitrary",) * 3,
                                             vmem_limit_bytes=VMEM_LIMIT),
        name="attn_prompt",
    )(q, k, v)


def _attn_sample_body(geom, pt_ref, q_ref, qlat_ref, latn_ref, krn_ref, lat_hbm, krt_hbm, o_ref,
                      lat_buf, kr_buf, kb_buf, s_buf, st_ref, acc_ref, p_ref, sems):
    nseq, n_pages, ch, page, nope, rope = geom
    n_chunks = n_pages // ch
    n_steps = nseq * n_chunks
    n_heads = o_ref.shape[1]
    hp = q_ref.shape[1]
    n_slots = lat_buf.shape[0]
    n_kb = kb_buf.shape[0]

    def page_copies(pg, slot, j):
        cols = pl.ds(j * page, page)
        return (pltpu.make_async_copy(lat_hbm.at[pg], lat_buf.at[slot, cols], sems.at[0, slot]),
                pltpu.make_async_copy(krt_hbm.at[pg], kr_buf.at[slot, :, cols], sems.at[1, slot]))

    def start_chunk(g, slot):
        for j in range(ch):
            for cp in page_copies(pt_ref[g * ch + j], slot, j):
                cp.start()

    def wait_chunk(slot):
        pltpu.make_async_copy(lat_buf.at[slot], lat_buf.at[slot], sems.at[0, slot]).wait()
        pltpu.make_async_copy(kr_buf.at[slot], kr_buf.at[slot], sems.at[1, slot]).wait()

    def put(i, col):
        st_ref[i] = jnp.broadcast_to(col, st_ref.shape[1:])

    def substep(a, u):
        m, l, alpha = st_ref[0][:, 0:1], st_ref[1][:, 0:1], st_ref[2][:, 0:1]
        acc, p = acc_ref[...], p_ref[...]
        wait_chunk(u)

        acc = alpha * acc + _dot(p, kb_buf[(u - 2) % n_kb])
        out_row = jnp.minimum(lax.div(jnp.maximum(a - 2, 0), n_chunks), nseq)
        o_ref[out_row] = (acc / l)[0:n_heads, :]

        seq_a = jnp.minimum(lax.div(a, n_chunks), nseq - 1)
        refill = jnp.minimum(a + n_slots - 1, n_steps - 1) * ch
        rslot = (u + n_slots - 1) % n_slots
        pages = []
        for j in range(ch):
            rows = pl.ds(j * page, page)
            kbj = lat_buf[u, rows, :].astype(BF16)
            kb_buf[u % n_kb, rows, :] = kbj
            pages.append(kbj)
            for cp in page_copies(pt_ref[refill + j], rslot, j):
                cp.start()
        kb = jnp.concatenate(pages, axis=0)
        krb = kr_buf[u].astype(BF16)
        qa = q_ref[seq_a]
        s_buf[u % 2] = (_dot_nt(qlat_ref[seq_a].astype(BF16), kb)
                        + _dot(qa[:, nope:nope + rope].astype(BF16), krb))

        b = jnp.maximum(a - 1, 0)
        seq_b = jnp.minimum(lax.div(b, n_chunks), nseq - 1)
        first = lax.rem(b, n_chunks) == 0
        qb = q_ref[seq_b]
        qlat = qlat_ref[seq_b]
        latn = latn_ref[pl.ds(seq_b, 1), :]
        krn = krn_ref[pl.ds(seq_b, 1), :]
        s_new = (jnp.sum(qlat * latn, axis=-1, keepdims=True)
                 + jnp.sum(qb[:, nope:nope + rope] * krn, axis=-1, keepdims=True))
        m = jnp.where(first, s_new, m)
        l = jnp.where(first, 1.0, l)
        acc = jnp.where(first, jnp.broadcast_to(latn, acc.shape), acc)
        s = s_buf[(u - 1) % 2]
        m_new = jnp.maximum(m, jnp.max(s, axis=-1, keepdims=True))
        p = jnp.exp2(s - m_new)
        alpha = jnp.exp2(m - m_new)
        put(0, m_new)
        put(1, alpha * l + jnp.sum(p, axis=-1, keepdims=True))
        put(2, alpha)
        acc_ref[...] = acc
        p_ref[...] = p.astype(BF16)

    i = pl.program_id(0)

    @pl.when(i == 0)
    def _():
        for c in range(n_slots - 1):
            start_chunk(i + c, c)
        for k in range(2, n_kb):
            kb_buf[k] = jnp.zeros(kb_buf.shape[1:], BF16)
        s_buf[1] = jnp.zeros(s_buf.shape[1:], F32)
        p_ref[...] = jnp.zeros(p_ref.shape, BF16)
        acc_ref[...] = jnp.zeros(acc_ref.shape, F32)
        put(0, jnp.zeros((hp, 1), F32))
        put(1, jnp.ones((hp, 1), F32))
        put(2, jnp.zeros((hp, 1), F32))

    for u in range(n_slots):
        pl.when(i >= 0)(functools.partial(substep, n_slots * i + u, u))

    @pl.when(i == pl.num_programs(0) - 1)
    def _():
        for slot in range(n_slots - 1):
            wait_chunk(slot)


def _attn_sample(page_table, q16, qlat16, lat_new, kr_new, cache_lat, cache_krt, n_heads, nope,
                 ch=32, n_slots=4):
    nseq, n_pages = page_table.shape
    _, page, kv_rank = cache_lat.shape
    rope = cache_krt.shape[1]
    assert n_pages % ch == 0 and n_slots % 4 == 0
    chk = ch * page
    hp = q16.shape[1]
    geom = (nseq, n_pages, ch, page, nope, rope)
    positions = nseq * (n_pages // ch) + 2
    whole = lambda a: pl.BlockSpec(a.shape, lambda i, pt: (0,) * a.ndim)
    grid_spec = pltpu.PrefetchScalarGridSpec(
        num_scalar_prefetch=1,
        grid=(pl.cdiv(positions, n_slots),),
        in_specs=[whole(q16), whole(qlat16), whole(lat_new), whole(kr_new),
                  pl.BlockSpec(memory_space=pl.ANY), pl.BlockSpec(memory_space=pl.ANY)],
        out_specs=pl.BlockSpec((nseq + 1, n_heads, kv_rank), lambda i, pt: (0, 0, 0)),
        scratch_shapes=[pltpu.VMEM((n_slots, chk, kv_rank), F32),
                        pltpu.VMEM((n_slots, rope, chk), F32),
                        pltpu.VMEM((4, chk, kv_rank), BF16),
                        pltpu.VMEM((2, hp, chk), F32),
                        pltpu.VMEM((3, hp, LANES), F32),
                        pltpu.VMEM((hp, kv_rank), F32),
                        pltpu.VMEM((hp, chk), BF16),
                        pltpu.SemaphoreType.DMA((2, n_slots))],
    )
    out = pl.pallas_call(
        functools.partial(_attn_sample_body, geom),
        grid_spec=grid_spec,
        out_shape=jax.ShapeDtypeStruct((nseq + 1, n_heads, kv_rank), F32),
        compiler_params=pltpu.CompilerParams(dimension_semantics=("arbitrary",),
                                             vmem_limit_bytes=VMEM_LIMIT),
        name="attn_sample",
    )(page_table.reshape(-1), q16, qlat16, lat_new, kr_new, cache_lat, cache_krt)
    return out[:nseq]


def _post_body(alpha, ff_chunks, sample, *refs):
    if sample:
        (x_ref, convout_ref, olat_ref, ga_ref, shf_ref, scf_ref, gf_ref, woc_ref, woa_ref,
         ln1g_ref, ln1b_ref, wg_ref, wu_ref, wd_ref, ln2g_ref, ln2b_ref, wuv_ref, y_ref) = refs
    else:
        (x_ref, convout_ref, attn_ref, ga_ref, shf_ref, scf_ref, gf_ref, woc_ref, woa_ref,
         ln1g_ref, ln1b_ref, wg_ref, wu_ref, wd_ref, ln2g_ref, ln2b_ref, y_ref) = refs
    tm = x_ref.shape[1]

    def mod(ref, rows):
        return ref[0] if ref.shape[1] == 1 else ref[0, rows, :]

    sub = min(tm, POST_ROWS)
    chunks = [pl.ds(r0, sub) for r0 in range(0, tm, sub)]

    def out_proj(rows):
        if sample:
            attn = _dot(olat_ref[0, rows, :].astype(BF16), wuv_ref[...]).astype(BF16)
        else:
            attn = attn_ref[0, rows, :]
        return _dot(convout_ref[0, rows, :], woc_ref[...]) + _dot(attn, woa_ref[...])

    def norm1(rows, a):
        return _layer_norm(alpha * x_ref[0, rows, :] + (1.0 + mod(ga_ref, rows)) * a, ln1g_ref[...], ln1b_ref[...])

    def ffn(rows, x1):
        ub = (x1 * (1.0 + mod(scf_ref, rows)) + mod(shf_ref, rows)).astype(BF16)
        f = None
        for lo, hi in ff_chunks:
            g = _dot(ub, wg_ref[:, lo:hi])
            up = _dot(ub, wu_ref[:, lo:hi])
            hmid = (g * jax.nn.sigmoid(g) * up).astype(BF16)
            part = _dot(hmid, wd_ref[lo:hi, :])
            f = part if f is None else f + part
        return f

    def norm2(rows, x1, f):
        y_ref[0, rows, :] = _layer_norm(alpha * x1 + (1.0 + mod(gf_ref, rows)) * f, ln2g_ref[...], ln2b_ref[...])

    a = [out_proj(r) for r in chunks]
    x1 = [norm1(r, ai) for r, ai in zip(chunks, a)]
    f = [None] * len(chunks)
    for i, r in enumerate(chunks):
        f[i] = ffn(r, x1[i])
        if i > 0:
            norm2(chunks[i - 1], x1[i - 1], f[i - 1])
    norm2(chunks[-1], x1[-1], f[-1])


def _post(alpha, sample, tm, x, convout, attn, mods, weights, extra):
    nb, s, d = x.shape
    r = mods[0].shape[1]
    d_ff = weights[5].shape[1]
    cut = min(d_ff, pl.cdiv(d_ff // 2, MXU_TILE) * MXU_TILE)
    ff_chunks = tuple(c for c in ((0, cut), (cut, d_ff)) if c[1] > c[0])
    row = lambda b, i: (b, i, 0)
    mod_spec = pl.BlockSpec((1, r, d), lambda b, i: (b, 0, 0))
    in_specs = [pl.BlockSpec((1, tm, d), row),
                pl.BlockSpec((1, tm, convout.shape[2]), row),
                pl.BlockSpec((1, tm, attn.shape[2]), row),
                mod_spec, mod_spec, mod_spec, mod_spec]
    in_specs += [_const_spec(w.shape) for w in weights]
    in_specs += [_const_spec(e.shape) for e in extra]
    return pl.pallas_call(
        functools.partial(_post_body, alpha, ff_chunks, sample),
        grid=(nb, s // tm), in_specs=in_specs,
        out_specs=pl.BlockSpec((1, tm, d), row),
        out_shape=jax.ShapeDtypeStruct((nb, s, d), F32),
        compiler_params=pltpu.CompilerParams(dimension_semantics=("arbitrary", "arbitrary"),
                                             vmem_limit_bytes=VMEM_LIMIT),
        name="post_sample" if sample else "post_prompt",
    )(x, convout, attn, *mods, *weights, *extra)


def _rope_tables(pos, rope, nope):
    half = rope // 2
    inv = 1.0 / (ROPE_THETA ** (jnp.arange(0, rope, 2, dtype=F32) / rope))
    ang = pos.astype(F32)[:, None] * inv[None, :]
    cos, sin = jnp.cos(ang), jnp.sin(ang)
    n = pos.shape[0]
    z = lambda w: jnp.zeros((n, w), F32)
    pad = LANES - nope - rope
    ta = jnp.concatenate([jnp.ones((n, nope), F32), cos, cos, z(pad)], axis=1)
    tb = jnp.concatenate([z(nope + half), sin, z(pad)], axis=1)
    tc = jnp.concatenate([z(nope), -sin, z(half + pad)], axis=1)
    return ta, tb, tc


def kernel(x_prompt, x_sample, cache_latent, cache_k_rope, state_conv, page_table, c_prompt, c_sample,
           w_ada, b_ada, w_in, conv_w, g_q, g_kv, w_uq, w_uk, w_uv, w_o, ln1_g, ln1_b, w_gate, w_up,
           w_down, ln2_g, ln2_b):
    depth = w_ada.shape[0]
    nb, seq, d = x_prompt.shape
    ns, dec_seq, _ = x_sample.shape
    assert dec_seq == 1
    conv_dim = conv_w.shape[2]
    q_rank = g_q.shape[1]
    kv_rank = g_kv.shape[1]
    n_heads, nope = w_uk.shape[2], w_uk.shape[3]
    rope = w_uq.shape[3] - nope
    v_dim = w_uv.shape[3]
    half = rope // 2
    page = cache_latent.shape[2]
    past_len = page_table.shape[1] * page
    alpha = (2 * depth) ** 0.25
    scale = (nope + rope) ** -0.5 * LOG2E
    dims = (conv_dim, q_rank, kv_rank, n_heads, nope, half, scale)
    pad = HEAD_PAD - nope - rope
    assert pad >= 0 and 2 * v_dim == HEAD_PAD

    tabs_p = _rope_tables(jnp.arange(seq), rope, nope)
    tabs_s = _rope_tables(jnp.full((ns,), past_len), rope, nope)

    xp = x_prompt
    xs = x_sample.reshape(1, ns, d)
    c_all = jnp.concatenate([c_prompt, c_sample], axis=0)
    c_all = jnp.pad(c_all, ((0, -(nb + ns) % (2 * SUBLANES)), (0, 0)))
    outs = [[] for _ in range(6)]
    for l in range(depth):
        w_in_l = w_in[l]
        c4 = 3 * conv_dim + q_rank
        c5 = c4 + kv_rank
        wkr = jnp.pad(w_in_l[:, c5:], ((0, 0), (nope, pad)))
        wmain = jnp.concatenate([w_in_l[:, :c4], wkr, w_in_l[:, c4:c5]], axis=1).astype(BF16)
        wq = jnp.pad(w_uq[l], ((0, 0), (0, 0), (0, pad))).reshape(q_rank, n_heads * HEAD_PAD).astype(BF16)
        wk = jnp.pad(w_uk[l], ((0, 0), (0, 0), (0, HEAD_PAD - nope))).reshape(kv_rank, n_heads * HEAD_PAD)
        wk = wk.astype(BF16)
        wv = w_uv[l].reshape(kv_rank, n_heads * v_dim).astype(BF16)
        wukt = jnp.transpose(w_uk[l], (1, 2, 0)).astype(BF16)
        eye = jnp.eye(n_heads, dtype=F32)
        wuv_bd = (w_uv[l].transpose(1, 0, 2)[:, :, None, :] * eye[:, None, :, None])
        wuv_bd = wuv_bd.reshape(n_heads * kv_rank, n_heads * v_dim).astype(BF16)
        woc = w_o[l, :conv_dim].astype(BF16)
        woa = w_o[l, conv_dim:].astype(BF16)
        proj_w = (wmain, conv_w[l], g_q[l].reshape(1, -1), g_kv[l].reshape(1, -1), wq)
        post_w = (woc, woa, ln1_g[l].reshape(1, -1), ln1_b[l].reshape(1, -1), w_gate[l].astype(BF16),
                  w_up[l].astype(BF16), w_down[l].astype(BF16), ln2_g[l].reshape(1, -1),
                  ln2_b[l].reshape(1, -1))

        mod = _ada(c_all, w_ada[l], b_ada[l])
        mp = [mod[:nb, i * d:(i + 1) * d].reshape(nb, 1, d) for i in range(6)]
        ms = [mod[nb:nb + ns, i * d:(i + 1) * d].reshape(1, ns, d) for i in range(6)]

        convout, q, k, v, lat_p, kr_p, tail = _proj(dims, False, 1024, xp, mp[0], mp[1], tabs_p, proj_w,
                                                     (wk, wv))
        attn = _attn_prompt(q, k, v, n_heads)
        xp = _post(alpha, False, 1024, xp, convout, attn, (mp[2], mp[3], mp[4], mp[5]), post_w, ())

        st = state_conv[l]
        convout_s, q_s, qlat_s, lat_s, kr_s, cin_s = _proj(
            dims, True, ns, xs, ms[0], ms[1], tabs_s, proj_w, (st[:, 0], st[:, 1], wukt))
        rows_pad = ((0, 0), (0, 2 * SUBLANES - n_heads), (0, 0))
        q16 = jnp.pad(q_s.reshape(ns, n_heads, HEAD_PAD), rows_pad)
        qlat16 = jnp.pad(qlat_s.reshape(ns, n_heads, kv_rank), rows_pad)
        cache_krt = jnp.swapaxes(cache_k_rope[l], 1, 2)
        olat = _attn_sample(page_table, q16, qlat16, lat_s[0], kr_s[0], cache_latent[l], cache_krt,
                            n_heads, nope)
        xs = _post(alpha, True, ns, xs, convout_s, olat.reshape(1, ns, n_heads * kv_rank),
                   (ms[2], ms[3], ms[4], ms[5]), post_w, (wuv_bd,))

        outs[0].append(lat_p)
        outs[1].append(kr_p)
        outs[2].append(tail[:, SUBLANES - 2:, :])
        outs[3].append(lat_s.reshape(ns, 1, kv_rank))
        outs[4].append(kr_s.reshape(ns, 1, rope))
        outs[5].append(jnp.stack([st[:, 1], cin_s[0]], axis=1))
    return (xp, xs.reshape(ns, 1, d), jnp.stack(outs[0]), jnp.stack(outs[1]), jnp.stack(outs[2]),
            jnp.stack(outs[3]), jnp.stack(outs[4]), jnp.stack(outs[5]))
```

```python
import functools
import math

import jax
import jax.numpy as jnp
from jax import lax
from jax.experimental import pallas as pl
from jax.experimental.pallas import tpu as pltpu

F32 = jnp.float32
BF16 = jnp.bfloat16

ROPE_THETA = 10000.0
LN_EPS = 1e-5
RMS_EPS = 1e-6
NEG_INF = -1e30
LOG2E = math.log2(math.e)

LANES = 128
SUBLANES = 8
MXU_TILE = 256
PROJ_ROWS = 512
POST_ROWS = 512
HEAD_PAD = 128
VMEM_LIMIT = 56 * 1024 * 1024


def _dot(a, b):
    return jnp.dot(a, b, preferred_element_type=F32)


def _dot_nt(a, b):
    return lax.dot_general(a, b, (((1,), (1,)), ((), ())), preferred_element_type=F32)


def _layer_norm(r, g, b):
    mu = jnp.mean(r, axis=-1, keepdims=True)
    d = r - mu
    var = jnp.mean(d * d, axis=-1, keepdims=True)
    return d * lax.rsqrt(var + LN_EPS) * g + b


def _rms_norm(x, g):
    return x * lax.rsqrt(jnp.mean(x * x, axis=-1, keepdims=True) + RMS_EPS) * g


def _rope_group(x, ta, tb, tc, half):
    return x * ta + pltpu.roll(x, half, 1) * tb + pltpu.roll(x, LANES - half, 1) * tc


def _ada_body(c_ref, w_ref, b_ref, o_ref):
    c = c_ref[...]
    s = c * jax.nn.sigmoid(c)
    o_ref[...] = _dot(s.astype(BF16), w_ref[...].astype(BF16)) + b_ref[...]


def _ada(c_all, w_ada, b_ada, tn=1024):
    m, d = c_all.shape
    n = w_ada.shape[1]
    return pl.pallas_call(
        _ada_body,
        grid=(n // tn,),
        in_specs=[pl.BlockSpec((m, d), lambda j: (0, 0)),
                  pl.BlockSpec((d, tn), lambda j: (0, j)),
                  pl.BlockSpec((1, tn), lambda j: (0, j))],
        out_specs=pl.BlockSpec((m, tn), lambda j: (0, j)),
        out_shape=jax.ShapeDtypeStruct((m, n), F32),
        compiler_params=pltpu.CompilerParams(dimension_semantics=("arbitrary",),
                                             vmem_limit_bytes=VMEM_LIMIT),
        name="ada",
    )(c_all, w_ada, b_ada.reshape(1, n))


def _proj_body(dims, sample, *refs):
    conv_dim, q_rank, kv_rank, n_heads, nope, half, scale = dims
    if sample:
        (x_ref, sh_ref, sc_ref, ta_ref, tb_ref, tc_ref, wmain_ref, convw_ref, gq_ref, gkv_ref,
         wq_ref, s0_ref, s1_ref, wukt_ref,
         convout_ref, q_ref, qlat_ref, lat_ref, krot_ref, cin_ref) = refs
    else:
        (x_ref, sh_ref, sc_ref, ta_ref, tb_ref, tc_ref, wmain_ref, convw_ref, gq_ref, gkv_ref,
         wq_ref, wk_ref, wv_ref,
         convout_ref, q_ref, k_ref, v_ref, lat_ref, krot_ref, tail_ref, cin_buf) = refs
    tm = x_ref.shape[1]
    c1, c2, c3 = conv_dim, 2 * conv_dim, 3 * conv_dim
    c4 = c3 + q_rank + LANES
    c5 = c4 + kv_rank
    w0 = convw_ref[0:1, :]
    w1 = convw_ref[1:2, :]
    w2 = convw_ref[2:3, :]
    if not sample:
        @pl.when(pl.program_id(1) == 0)
        def _():
            cin_buf[0:SUBLANES, :] = jnp.zeros((SUBLANES, conv_dim), F32)

    def in_proj(rows):
        x = x_ref[0, rows, :]
        sc = sc_ref[0] if sc_ref.shape[1] == 1 else sc_ref[0, rows, :]
        sh = sh_ref[0] if sh_ref.shape[1] == 1 else sh_ref[0, rows, :]
        ub = (x * (1.0 + sc) + sh).astype(BF16)
        return (_dot(ub, wmain_ref[:, 0:c1]), _dot(ub, wmain_ref[:, c1:c2]), _dot(ub, wmain_ref[:, c2:c3]),
                _dot(ub, wmain_ref[:, c3:c4]), _dot(ub, wmain_ref[:, c4:c5]))

    def rest(rows, sub, products):
        h, gb, gc, cq_kr, ckv = products
        cq = cq_kr[:, 0:q_rank]
        kr = cq_kr[:, q_rank:q_rank + LANES]

        conv_in = gc * h
        if sample:
            y = w0 * s0_ref[rows, :] + w1 * s1_ref[rows, :] + w2 * conv_in
            cin_ref[0, rows, :] = conv_in
        else:
            cin_buf[SUBLANES:SUBLANES + sub, :] = conv_in
            y = (w0 * cin_buf[SUBLANES - 2:SUBLANES - 2 + sub, :]
                 + w1 * cin_buf[SUBLANES - 1:SUBLANES - 1 + sub, :] + w2 * conv_in)
            tail = conv_in[sub - SUBLANES:sub, :]
            cin_buf[0:SUBLANES, :] = tail
            tail_ref[0] = tail
        convout_ref[0, rows, :] = (gb * y).astype(BF16)

        ta = ta_ref[rows, :]
        tb = tb_ref[rows, :]
        tc = tc_ref[rows, :]
        cqn = _rms_norm(cq, gq_ref[...]).astype(BF16)
        qf = _dot(cqn, wq_ref[...])
        q_heads = [_rope_group(qf[:, HEAD_PAD * i:HEAD_PAD * (i + 1)], ta, tb, tc, half) * scale
                   for i in range(n_heads)]
        q_ref[0, rows, :] = jnp.concatenate(q_heads, axis=1).astype(q_ref.dtype)

        latent = _rms_norm(ckv, gkv_ref[...])
        lat_ref[0, rows, :] = latent
        krot = _rope_group(kr, ta, tb, tc, half)
        krot_ref[0, rows, :] = krot[:, nope:nope + 2 * half]

        if sample:
            for i in range(n_heads):
                qn = q_heads[i][:, 0:nope].astype(BF16)
                qlat_ref[0, rows, kv_rank * i:kv_rank * (i + 1)] = _dot(qn, wukt_ref[i])
        else:
            latb = latent.astype(BF16)
            kf = _dot(latb, wk_ref[...])
            k_ref[0, rows, :] = (kf + jnp.concatenate([krot] * n_heads, axis=1)).astype(BF16)
            v_ref[0, rows, :] = _dot(latb, wv_ref[...]).astype(BF16)

    sub = min(tm, PROJ_ROWS)
    chunks = [pl.ds(r0, sub) for r0 in range(0, tm, sub)]
    products = [in_proj(rows) for rows in chunks]
    for rows, prod in zip(chunks, products):
        rest(rows, sub, prod)


def _const_spec(shape):
    nd = len(shape)
    return pl.BlockSpec(shape, lambda *_: (0,) * nd, pipeline_mode=pl.Buffered(1))


def _proj(dims, sample, tm, x, sh, sc, tabs, weights, extra):
    conv_dim, q_rank, kv_rank, n_heads, nope, half, _ = dims
    nb, s, d = x.shape
    r = sh.shape[1]
    grid = (nb, s // tm)
    row = lambda b, i: (b, i, 0)
    mod_spec = pl.BlockSpec((1, r, d), (lambda b, i: (b, 0, 0)))
    tab_spec = pl.BlockSpec((tm, LANES), lambda b, i: (i, 0))
    wmain, convw, gq, gkv, wq = weights
    in_specs = [pl.BlockSpec((1, tm, d), row), mod_spec, mod_spec, tab_spec, tab_spec, tab_spec,
                _const_spec(wmain.shape), _const_spec(convw.shape),
                _const_spec(gq.shape), _const_spec(gkv.shape), _const_spec(wq.shape)]
    in_specs += [_const_spec(e.shape) for e in extra]
    hq = n_heads * HEAD_PAD
    out_shape = [jax.ShapeDtypeStruct((nb, s, conv_dim), BF16)]
    out_specs = [pl.BlockSpec((1, tm, conv_dim), row)]
    if sample:
        out_shape += [jax.ShapeDtypeStruct((nb, s, hq), F32),
                      jax.ShapeDtypeStruct((nb, s, n_heads * kv_rank), F32)]
        out_specs += [pl.BlockSpec((1, tm, hq), row), pl.BlockSpec((1, tm, n_heads * kv_rank), row)]
    else:
        out_shape += [jax.ShapeDtypeStruct((nb, s, hq), BF16), jax.ShapeDtypeStruct((nb, s, hq), BF16),
                      jax.ShapeDtypeStruct((nb, s, n_heads * (hq // n_heads // 2)), BF16)]
        out_specs += [pl.BlockSpec((1, tm, hq), row), pl.BlockSpec((1, tm, hq), row),
                      pl.BlockSpec((1, tm, n_heads * (hq // n_heads // 2)), row)]
    out_shape += [jax.ShapeDtypeStruct((nb, s, kv_rank), F32), jax.ShapeDtypeStruct((nb, s, 2 * half), F32)]
    out_specs += [pl.BlockSpec((1, tm, kv_rank), row), pl.BlockSpec((1, tm, 2 * half), row)]
    scratch = []
    if sample:
        out_shape += [jax.ShapeDtypeStruct((nb, s, conv_dim), F32)]
        out_specs += [pl.BlockSpec((1, tm, conv_dim), row)]
    else:
        out_shape += [jax.ShapeDtypeStruct((nb, SUBLANES, conv_dim), F32)]
        out_specs += [pl.BlockSpec((1, SUBLANES, conv_dim), lambda b, i: (b, 0, 0))]
        scratch = [pltpu.VMEM((SUBLANES + tm, conv_dim), F32)]
    return pl.pallas_call(
        functools.partial(_proj_body, dims, sample),
        grid=grid, in_specs=in_specs, out_specs=out_specs, out_shape=out_shape,
        scratch_shapes=scratch,
        compiler_params=pltpu.CompilerParams(dimension_semantics=("arbitrary", "arbitrary"),
                                             vmem_limit_bytes=VMEM_LIMIT),
        name="proj_sample" if sample else "proj_prompt",
    )(x, sh, sc, *tabs, *weights, *extra)


def _attn_prompt_body(v_dim, q_ref, k_ref, v_ref, o_ref, vt_ref):
    tq = q_ref.shape[1]
    tk = tq
    qi = pl.program_id(2)

    @pl.when(qi == 0)
    def _():
        vt_ref[...] = v_ref[0].astype(F32).T.astype(BF16)

    tw = tq
    n_w = tq // tw
    keys = lax.broadcasted_iota(jnp.int32, (tk, tw), 0)
    queries = lax.broadcasted_iota(jnp.int32, (tk, tw), 1)
    n_h = q_ref.shape[2] // HEAD_PAD
    qts = [q_ref[0, :, HEAD_PAD * hh:HEAD_PAD * (hh + 1)].astype(F32).T.astype(BF16) for hh in range(n_h)]

    def step(j, carry, masked):
        at = pl.ds(pl.multiple_of(j * tk, tk), tk)

        def scores(i):
            hh, c = divmod(i, n_w)
            k = k_ref[0, at, HEAD_PAD * hh:HEAD_PAD * (hh + 1)]
            s = _dot(k, qts[hh][:, tw * c:tw * (c + 1)])
            return jnp.where(keys <= queries + tw * c, s, NEG_INF) if masked else s

        new = []
        s_next = scores(0)
        for i, (m, l, acc) in enumerate(carry):
            hh = i // n_w
            vt = vt_ref[v_dim * hh:v_dim * (hh + 1), at]
            s = s_next
            if i + 1 < len(carry):
                s_next = scores(i + 1)
            m_new = jnp.maximum(m, jnp.max(s, axis=0, keepdims=True))
            p = jnp.exp2(s - m_new)
            alpha = jnp.exp2(m - m_new)
            l = alpha * l + jnp.sum(p, axis=0, keepdims=True)
            acc = alpha * acc + _dot(vt, p.astype(BF16))
            new.append((m_new, l, acc))
        return tuple(new)

    init = tuple((jnp.full((1, tw), NEG_INF, F32), jnp.zeros((1, tw), F32), jnp.zeros((v_dim, tw), F32))
                 for _ in range(n_h * n_w))
    carry = lax.fori_loop(0, qi, functools.partial(step, masked=False), init)
    carry = step(qi, carry, True)
    heads = [jnp.concatenate([acc / l for _, l, acc in carry[n_w * hh:n_w * (hh + 1)]], axis=1)
             for hh in range(n_h)]
    o_ref[0] = jnp.concatenate(heads, axis=0).T.astype(o_ref.dtype)


def _attn_prompt(q, k, v, n_heads, tq=512, hg=8):
    b, s, _ = q.shape
    v_dim = v.shape[2] // n_heads
    assert n_heads % hg == 0 and (hg * v_dim) % LANES == 0
    return pl.pallas_call(
        functools.partial(_attn_prompt_body, v_dim),
        grid=(b, n_heads // hg, s // tq),
        in_specs=[pl.BlockSpec((1, tq, hg * HEAD_PAD), lambda bi, hp, qi: (bi, qi, hp)),
                  pl.BlockSpec((1, s, hg * HEAD_PAD), lambda bi, hp, qi: (bi, 0, hp)),
                  pl.BlockSpec((1, s, hg * v_dim), lambda bi, hp, qi: (bi, 0, hp))],
        out_specs=pl.BlockSpec((1, tq, hg * v_dim), lambda bi, hp, qi: (bi, qi, hp)),
        out_shape=jax.ShapeDtypeStruct((b, s, n_heads * v_dim), BF16),
        scratch_shapes=[pltpu.VMEM((hg * v_dim, s), BF16)],
        compiler_params=pltpu.CompilerParams(dimension_semantics=("arbitrary",) * 3,
                                             vmem_limit_bytes=VMEM_LIMIT),
        name="attn_prompt",
    )(q, k, v)


def _attn_sample_body(geom, pt_ref, q_ref, qlat_ref, latn_ref, krn_ref, lat_hbm, krt_hbm, o_ref,
                      lat_buf, kr_buf, kb_buf, s_buf, st_ref, acc_ref, p_ref, sems):
    nseq, n_pages, ch, page, nope, rope = geom
    n_chunks = n_pages // ch
    n_steps = nseq * n_chunks
    n_heads = o_ref.shape[1]
    hp = q_ref.shape[1]
    n_slots = lat_buf.shape[0]
    n_kb = kb_buf.shape[0]

    def page_copies(pg, slot, j):
        cols = pl.ds(j * page, page)
        return (pltpu.make_async_copy(lat_hbm.at[pg], lat_buf.at[slot, cols], sems.at[0, slot]),
                pltpu.make_async_copy(krt_hbm.at[pg], kr_buf.at[slot, :, cols], sems.at[1, slot]))

    def start_chunk(g, slot):
        for j in range(ch):
            for cp in page_copies(pt_ref[g * ch + j], slot, j):
                cp.start()

    def wait_chunk(slot):
        pltpu.make_async_copy(lat_buf.at[slot], lat_buf.at[slot], sems.at[0, slot]).wait()
        pltpu.make_async_copy(kr_buf.at[slot], kr_buf.at[slot], sems.at[1, slot]).wait()

    def put(i, col):
        st_ref[i] = jnp.broadcast_to(col, st_ref.shape[1:])

    def substep(a, u):
        m, l, alpha = st_ref[0][:, 0:1], st_ref[1][:, 0:1], st_ref[2][:, 0:1]
        acc, p = acc_ref[...], p_ref[...]
        wait_chunk(u)

        acc = alpha * acc + _dot(p, kb_buf[(u - 2) % n_kb])
        out_row = jnp.minimum(lax.div(jnp.maximum(a - 2, 0), n_chunks), nseq)
        o_ref[out_row] = (acc / l)[0:n_heads, :]

        seq_a = jnp.minimum(lax.div(a, n_chunks), nseq - 1)
        refill = jnp.minimum(a + n_slots - 1, n_steps - 1) * ch
        rslot = (u + n_slots - 1) % n_slots
        pages = []
        for j in range(ch):
            rows = pl.ds(j * page, page)
            kbj = lat_buf[u, rows, :].astype(BF16)
            kb_buf[u % n_kb, rows, :] = kbj
            pages.append(kbj)
            for cp in page_copies(pt_ref[refill + j], rslot, j):
                cp.start()
        kb = jnp.concatenate(pages, axis=0)
        krb = kr_buf[u].astype(BF16)
        qa = q_ref[seq_a]
        s_buf[u % 2] = (_dot_nt(qlat_ref[seq_a].astype(BF16), kb)
                        + _dot(qa[:, nope:nope + rope].astype(BF16), krb))

        b = jnp.maximum(a - 1, 0)
        seq_b = jnp.minimum(lax.div(b, n_chunks), nseq - 1)
        first = lax.rem(b, n_chunks) == 0
        qb = q_ref[seq_b]
        qlat = qlat_ref[seq_b]
        latn = latn_ref[pl.ds(seq_b, 1), :]
        krn = krn_ref[pl.ds(seq_b, 1), :]
        s_new = (jnp.sum(qlat * latn, axis=-1, keepdims=True)
                 + jnp.sum(qb[:, nope:nope + rope] * krn, axis=-1, keepdims=True))
        m = jnp.where(first, s_new, m)
        l = jnp.where(first, 1.0, l)
        acc = jnp.where(first, jnp.broadcast_to(latn, acc.shape), acc)
        s = s_buf[(u - 1) % 2]
        m_new = jnp.maximum(m, jnp.max(s, axis=-1, keepdims=True))
        p = jnp.exp2(s - m_new)
        alpha = jnp.exp2(m - m_new)
        put(0, m_new)
        put(1, alpha * l + jnp.sum(p, axis=-1, keepdims=True))
        put(2, alpha)
        acc_ref[...] = acc
        p_ref[...] = p.astype(BF16)

    i = pl.program_id(0)

    @pl.when(i == 0)
    def _():
        for c in range(n_slots - 1):
            start_chunk(i + c, c)
        for k in range(2, n_kb):
            kb_buf[k] = jnp.zeros(kb_buf.shape[1:], BF16)
        s_buf[1] = jnp.zeros(s_buf.shape[1:], F32)
        p_ref[...] = jnp.zeros(p_ref.shape, BF16)
        acc_ref[...] = jnp.zeros(acc_ref.shape, F32)
        put(0, jnp.zeros((hp, 1), F32))
        put(1, jnp.ones((hp, 1), F32))
        put(2, jnp.zeros((hp, 1), F32))

    for u in range(n_slots):
        pl.when(i >= 0)(functools.partial(substep, n_slots * i + u, u))

    @pl.when(i == pl.num_programs(0) - 1)
    def _():
        for slot in range(n_slots - 1):
            wait_chunk(slot)


def _attn_sample(page_table, q16, qlat16, lat_new, kr_new, cache_lat, cache_krt, n_heads, nope,
                 ch=32, n_slots=4):
    nseq, n_pages = page_table.shape
    _, page, kv_rank = cache_lat.shape
    rope = cache_krt.shape[1]
    assert n_pages % ch == 0 and n_slots % 4 == 0
    chk = ch * page
    hp = q16.shape[1]
    geom = (nseq, n_pages, ch, page, nope, rope)
    positions = nseq * (n_pages // ch) + 2
    whole = lambda a: pl.BlockSpec(a.shape, lambda i, pt: (0,) * a.ndim)
    grid_spec = pltpu.PrefetchScalarGridSpec(
        num_scalar_prefetch=1,
        grid=(pl.cdiv(positions, n_slots),),
        in_specs=[whole(q16), whole(qlat16), whole(lat_new), whole(kr_new),
                  pl.BlockSpec(memory_space=pl.ANY), pl.BlockSpec(memory_space=pl.ANY)],
        out_specs=pl.BlockSpec((nseq + 1, n_heads, kv_rank), lambda i, pt: (0, 0, 0)),
        scratch_shapes=[pltpu.VMEM((n_slots, chk, kv_rank), F32),
                        pltpu.VMEM((n_slots, rope, chk), F32),
                        pltpu.VMEM((4, chk, kv_rank), BF16),
                        pltpu.VMEM((2, hp, chk), F32),
                        pltpu.VMEM((3, hp, LANES), F32),
                        pltpu.VMEM((hp, kv_rank), F32),
                        pltpu.VMEM((hp, chk), BF16),
                        pltpu.SemaphoreType.DMA((2, n_slots))],
    )
    out = pl.pallas_call(
        functools.partial(_attn_sample_body, geom),
        grid_spec=grid_spec,
        out_shape=jax.ShapeDtypeStruct((nseq + 1, n_heads, kv_rank), F32),
        compiler_params=pltpu.CompilerParams(dimension_semantics=("arbitrary",),
                                             vmem_limit_bytes=VMEM_LIMIT),
        name="attn_sample",
    )(page_table.reshape(-1), q16, qlat16, lat_new, kr_new, cache_lat, cache_krt)
    return out[:nseq]


def _post_body(alpha, ff_chunks, sample, *refs):
    if sample:
        (x_ref, convout_ref, olat_ref, ga_ref, shf_ref, scf_ref, gf_ref, woc_ref, woa_ref,
         ln1g_ref, ln1b_ref, wg_ref, wu_ref, wd_ref, ln2g_ref, ln2b_ref, wuv_ref, y_ref) = refs
    else:
        (x_ref, convout_ref, attn_ref, ga_ref, shf_ref, scf_ref, gf_ref, woc_ref, woa_ref,
         ln1g_ref, ln1b_ref, wg_ref, wu_ref, wd_ref, ln2g_ref, ln2b_ref, y_ref) = refs
    tm = x_ref.shape[1]

    def mod(ref, rows):
        return ref[0] if ref.shape[1] == 1 else ref[0, rows, :]

    sub = min(tm, POST_ROWS)
    chunks = [pl.ds(r0, sub) for r0 in range(0, tm, sub)]

    def out_proj(rows):
        if sample:
            attn = _dot(olat_ref[0, rows, :].astype(BF16), wuv_ref[...]).astype(BF16)
        else:
            attn = attn_ref[0, rows, :]
        return _dot(convout_ref[0, rows, :], woc_ref[...]) + _dot(attn, woa_ref[...])

    def norm1(rows, a):
        return _layer_norm(alpha * x_ref[0, rows, :] + (1.0 + mod(ga_ref, rows)) * a, ln1g_ref[...], ln1b_ref[...])

    def ffn(rows, x1):
        ub = (x1 * (1.0 + mod(scf_ref, rows)) + mod(shf_ref, rows)).astype(BF16)
        gu = [(_dot(ub, wg_ref[:, lo:hi]), _dot(ub, wu_ref[:, lo:hi])) for lo, hi in ff_chunks]
        f = None
        for (lo, hi), (g, up) in zip(ff_chunks, gu):
            hmid = (g * jax.nn.sigmoid(g) * up).astype(BF16)
            part = _dot(hmid, wd_ref[lo:hi, :])
            f = part if f is None else f + part
        return f

    def norm2(rows, x1, f):
        y_ref[0, rows, :] = _layer_norm(alpha * x1 + (1.0 + mod(gf_ref, rows)) * f, ln2g_ref[...], ln2b_ref[...])

    a = [out_proj(r) for r in chunks]
    x1 = [norm1(r, ai) for r, ai in zip(chunks, a)]
    f = [None] * len(chunks)
    for i, r in enumerate(chunks):
        f[i] = ffn(r, x1[i])
        if i > 0:
            norm2(chunks[i - 1], x1[i - 1], f[i - 1])
    norm2(chunks[-1], x1[-1], f[-1])


def _post(alpha, sample, tm, x, convout, attn, mods, weights, extra):
    nb, s, d = x.shape
    r = mods[0].shape[1]
    d_ff = weights[5].shape[1]
    cut = min(d_ff, pl.cdiv(d_ff // 2, MXU_TILE) * MXU_TILE)
    ff_chunks = tuple(c for c in ((0, cut), (cut, d_ff)) if c[1] > c[0])
    row = lambda b, i: (b, i, 0)
    mod_spec = pl.BlockSpec((1, r, d), lambda b, i: (b, 0, 0))
    in_specs = [pl.BlockSpec((1, tm, d), row),
                pl.BlockSpec((1, tm, convout.shape[2]), row),
                pl.BlockSpec((1, tm, attn.shape[2]), row),
                mod_spec, mod_spec, mod_spec, mod_spec]
    in_specs += [_const_spec(w.shape) for w in weights]
    in_specs += [_const_spec(e.shape) for e in extra]
    return pl.pallas_call(
        functools.partial(_post_body, alpha, ff_chunks, sample),
        grid=(nb, s // tm), in_specs=in_specs,
        out_specs=pl.BlockSpec((1, tm, d), row),
        out_shape=jax.ShapeDtypeStruct((nb, s, d), F32),
        compiler_params=pltpu.CompilerParams(dimension_semantics=("arbitrary", "arbitrary"),
                                             vmem_limit_bytes=VMEM_LIMIT),
        name="post_sample" if sample else "post_prompt",
    )(x, convout, attn, *mods, *weights, *extra)


def _rope_tables(pos, rope, nope):
    half = rope // 2
    inv = 1.0 / (ROPE_THETA ** (jnp.arange(0, rope, 2, dtype=F32) / rope))
    ang = pos.astype(F32)[:, None] * inv[None, :]
    cos, sin = jnp.cos(ang), jnp.sin(ang)
    n = pos.shape[0]
    z = lambda w: jnp.zeros((n, w), F32)
    pad = LANES - nope - rope
    ta = jnp.concatenate([jnp.ones((n, nope), F32), cos, cos, z(pad)], axis=1)
    tb = jnp.concatenate([z(nope + half), sin, z(pad)], axis=1)
    tc = jnp.concatenate([z(nope), -sin, z(half + pad)], axis=1)
    return ta, tb, tc


def kernel(x_prompt, x_sample, cache_latent, cache_k_rope, state_conv, page_table, c_prompt, c_sample,
           w_ada, b_ada, w_in, conv_w, g_q, g_kv, w_uq, w_uk, w_uv, w_o, ln1_g, ln1_b, w_gate, w_up,
           w_down, ln2_g, ln2_b):
    depth = w_ada.shape[0]
    nb, seq, d = x_prompt.shape
    ns, dec_seq, _ = x_sample.shape
    assert dec_seq == 1
    conv_dim = conv_w.shape[2]
    q_rank = g_q.shape[1]
    kv_rank = g_kv.shape[1]
    n_heads, nope = w_uk.shape[2], w_uk.shape[3]
    rope = w_uq.shape[3] - nope
    v_dim = w_uv.shape[3]
    half = rope // 2
    page = cache_latent.shape[2]
    past_len = page_table.shape[1] * page
    alpha = (2 * depth) ** 0.25
    scale = (nope + rope) ** -0.5 * LOG2E
    dims = (conv_dim, q_rank, kv_rank, n_heads, nope, half, scale)
    pad = HEAD_PAD - nope - rope
    assert pad >= 0 and 2 * v_dim == HEAD_PAD

    tabs_p = _rope_tables(jnp.arange(seq), rope, nope)
    tabs_s = _rope_tables(jnp.full((ns,), past_len), rope, nope)

    xp = x_prompt
    xs = x_sample.reshape(1, ns, d)
    c_all = jnp.concatenate([c_prompt, c_sample], axis=0)
    c_all = jnp.pad(c_all, ((0, -(nb + ns) % (2 * SUBLANES)), (0, 0)))
    outs = [[] for _ in range(6)]
    for l in range(depth):
        w_in_l = w_in[l]
        c4 = 3 * conv_dim + q_rank
        c5 = c4 + kv_rank
        wkr = jnp.pad(w_in_l[:, c5:], ((0, 0), (nope, pad)))
        wmain = jnp.concatenate([w_in_l[:, :c4], wkr, w_in_l[:, c4:c5]], axis=1).astype(BF16)
        wq = jnp.pad(w_uq[l], ((0, 0), (0, 0), (0, pad))).reshape(q_rank, n_heads * HEAD_PAD).astype(BF16)
        wk = jnp.pad(w_uk[l], ((0, 0), (0, 0), (0, HEAD_PAD - nope))).reshape(kv_rank, n_heads * HEAD_PAD)
        wk = wk.astype(BF16)
        wv = w_uv[l].reshape(kv_rank, n_heads * v_dim).astype(BF16)
        wukt = jnp.transpose(w_uk[l], (1, 2, 0)).astype(BF16)
        eye = jnp.eye(n_heads, dtype=F32)
        wuv_bd = (w_uv[l].transpose(1, 0, 2)[:, :, None, :] * eye[:, None, :, None])
        wuv_bd = wuv_bd.reshape(n_heads * kv_rank, n_heads * v_dim).astype(BF16)
        woc = w_o[l, :conv_dim].astype(BF16)
        woa = w_o[l, conv_dim:].astype(BF16)
        proj_w = (wmain, conv_w[l], g_q[l].reshape(1, -1), g_kv[l].reshape(1, -1), wq)
        post_w = (woc, woa, ln1_g[l].reshape(1, -1), ln1_b[l].reshape(1, -1), w_gate[l].astype(BF16),
                  w_up[l].astype(BF16), w_down[l].astype(BF16), ln2_g[l].reshape(1, -1),
                  ln2_b[l].reshape(1, -1))

        mod = _ada(c_all, w_ada[l], b_ada[l])
        mp = [mod[:nb, i * d:(i + 1) * d].reshape(nb, 1, d) for i in range(6)]
        ms = [mod[nb:nb + ns, i * d:(i + 1) * d].reshape(1, ns, d) for i in range(6)]

        convout, q, k, v, lat_p, kr_p, tail = _proj(dims, False, 1024, xp, mp[0], mp[1], tabs_p, proj_w,
                                                     (wk, wv))
        attn = _attn_prompt(q, k, v, n_heads)
        xp = _post(alpha, False, 1024, xp, convout, attn, (mp[2], mp[3], mp[4], mp[5]), post_w, ())

        st = state_conv[l]
        convout_s, q_s, qlat_s, lat_s, kr_s, cin_s = _proj(
            dims, True, ns, xs, ms[0], ms[1], tabs_s, proj_w, (st[:, 0], st[:, 1], wukt))
        rows_pad = ((0, 0), (0, 2 * SUBLANES - n_heads), (0, 0))
        q16 = jnp.pad(q_s.reshape(ns, n_heads, HEAD_PAD), rows_pad)
        qlat16 = jnp.pad(qlat_s.reshape(ns, n_heads, kv_rank), rows_pad)
        cache_krt = jnp.swapaxes(cache_k_rope[l], 1, 2)
        olat = _attn_sample(page_table, q16, qlat16, lat_s[0], kr_s[0], cache_latent[l], cache_krt,
                            n_heads, nope)
        xs = _post(alpha, True, ns, xs, convout_s, olat.reshape(1, ns, n_heads * kv_rank),
                   (ms[2], ms[3], ms[4], ms[5]), post_w, (wuv_bd,))

        outs[0].append(lat_p)
        outs[1].append(kr_p)
        outs[2].append(tail[:, SUBLANES - 2:, :])
        outs[3].append(lat_s.reshape(ns, 1, kv_rank))
        outs[4].append(kr_s.reshape(ns, 1, rope))
        outs[5].append(jnp.stack([st[:, 1], cin_s[0]], axis=1))
    return (xp, xs.reshape(ns, 1, d), jnp.stack(outs[0]), jnp.stack(outs[1]), jnp.stack(outs[2]),
            jnp.stack(outs[3]), jnp.stack(outs[4]), jnp.stack(outs[5]))
```

```python
import functools
import math

import jax
import jax.numpy as jnp
from jax import lax
from jax.experimental import pallas as pl
from jax.experimental.pallas import tpu as pltpu

F32 = jnp.float32
BF16 = jnp.bfloat16

ROPE_THETA = 10000.0
LN_EPS = 1e-5
RMS_EPS = 1e-6
NEG_INF = -1e30
LOG2E = math.log2(math.e)

LANES = 128
SUBLANES = 8
MXU_TILE = 256
PROJ_ROWS = 512
POST_ROWS = 512
SKEW = 2
HEAD_PAD = 128
VMEM_LIMIT = 56 * 1024 * 1024


def _dot(a, b):
    return jnp.dot(a, b, preferred_element_type=F32)


def _dot_nt(a, b):
    return lax.dot_general(a, b, (((1,), (1,)), ((), ())), preferred_element_type=F32)


def _layer_norm(r, g, b):
    mu = jnp.mean(r, axis=-1, keepdims=True)
    d = r - mu
    var = jnp.mean(d * d, axis=-1, keepdims=True)
    return d * lax.rsqrt(var + LN_EPS) * g + b


def _rms_norm(x, g):
    return x * lax.rsqrt(jnp.mean(x * x, axis=-1, keepdims=True) + RMS_EPS) * g


def _rope_group(x, ta, tb, tc, half):
    return x * ta + pltpu.roll(x, half, 1) * tb + pltpu.roll(x, LANES - half, 1) * tc


def _ada_body(c_ref, w_ref, b_ref, o_ref):
    c = c_ref[...]
    s = c * jax.nn.sigmoid(c)
    o_ref[...] = _dot(s.astype(BF16), w_ref[...].astype(BF16)) + b_ref[...]


def _ada(c_all, w_ada, b_ada, tn=1024):
    m, d = c_all.shape
    n = w_ada.shape[1]
    return pl.pallas_call(
        _ada_body,
        grid=(n // tn,),
        in_specs=[pl.BlockSpec((m, d), lambda j: (0, 0)),
                  pl.BlockSpec((d, tn), lambda j: (0, j)),
                  pl.BlockSpec((1, tn), lambda j: (0, j))],
        out_specs=pl.BlockSpec((m, tn), lambda j: (0, j)),
        out_shape=jax.ShapeDtypeStruct((m, n), F32),
        compiler_params=pltpu.CompilerParams(dimension_semantics=("arbitrary",),
                                             vmem_limit_bytes=VMEM_LIMIT),
        name="ada",
    )(c_all, w_ada, b_ada.reshape(1, n))


def _proj_body(dims, sample, *refs):
    conv_dim, q_rank, kv_rank, n_heads, nope, half, scale = dims
    if sample:
        (x_ref, sh_ref, sc_ref, ta_ref, tb_ref, tc_ref, wmain_ref, convw_ref, gq_ref, gkv_ref,
         wq_ref, s0_ref, s1_ref, wukt_ref,
         convout_ref, q_ref, qlat_ref, lat_ref, krot_ref, cin_ref) = refs
    else:
        (x_ref, sh_ref, sc_ref, ta_ref, tb_ref, tc_ref, wmain_ref, convw_ref, gq_ref, gkv_ref,
         wq_ref, wk_ref, wv_ref,
         convout_ref, q_ref, k_ref, v_ref, lat_ref, krot_ref, tail_ref, cin_buf) = refs
    tm = x_ref.shape[1]
    c1, c2, c3 = conv_dim, 2 * conv_dim, 3 * conv_dim
    c4 = c3 + q_rank + LANES
    c5 = c4 + kv_rank
    w0 = convw_ref[0:1, :]
    w1 = convw_ref[1:2, :]
    w2 = convw_ref[2:3, :]
    if not sample:
        @pl.when(pl.program_id(1) == 0)
        def _():
            cin_buf[0:SUBLANES, :] = jnp.zeros((SUBLANES, conv_dim), F32)

    def in_proj(rows):
        x = x_ref[0, rows, :]
        sc = sc_ref[0] if sc_ref.shape[1] == 1 else sc_ref[0, rows, :]
        sh = sh_ref[0] if sh_ref.shape[1] == 1 else sh_ref[0, rows, :]
        ub = (x * (1.0 + sc) + sh).astype(BF16)
        return (_dot(ub, wmain_ref[:, 0:c1]), _dot(ub, wmain_ref[:, c1:c2]), _dot(ub, wmain_ref[:, c2:c3]),
                _dot(ub, wmain_ref[:, c3:c4]), _dot(ub, wmain_ref[:, c4:c5]))

    def rest(rows, sub, products):
        h, gb, gc, cq_kr, ckv = products
        cq = cq_kr[:, 0:q_rank]
        kr = cq_kr[:, q_rank:q_rank + LANES]

        conv_in = gc * h
        if sample:
            y = w0 * s0_ref[rows, :] + w1 * s1_ref[rows, :] + w2 * conv_in
            cin_ref[0, rows, :] = conv_in
        else:
            cin_buf[SUBLANES:SUBLANES + sub, :] = conv_in
            y = (w0 * cin_buf[SUBLANES - 2:SUBLANES - 2 + sub, :]
                 + w1 * cin_buf[SUBLANES - 1:SUBLANES - 1 + sub, :] + w2 * conv_in)
            tail = conv_in[sub - SUBLANES:sub, :]
            cin_buf[0:SUBLANES, :] = tail
            tail_ref[0] = tail
        convout_ref[0, rows, :] = (gb * y).astype(BF16)

        ta = ta_ref[rows, :]
        tb = tb_ref[rows, :]
        tc = tc_ref[rows, :]
        cqn = _rms_norm(cq, gq_ref[...]).astype(BF16)
        qf = _dot(cqn, wq_ref[...])
        q_heads = [_rope_group(qf[:, HEAD_PAD * i:HEAD_PAD * (i + 1)], ta, tb, tc, half) * scale
                   for i in range(n_heads)]
        q_ref[0, rows, :] = jnp.concatenate(q_heads, axis=1).astype(q_ref.dtype)

        latent = _rms_norm(ckv, gkv_ref[...])
        lat_ref[0, rows, :] = latent
        krot = _rope_group(kr, ta, tb, tc, half)
        krot_ref[0, rows, :] = krot[:, nope:nope + 2 * half]

        if sample:
            for i in range(n_heads):
                qn = q_heads[i][:, 0:nope].astype(BF16)
                qlat_ref[0, rows, kv_rank * i:kv_rank * (i + 1)] = _dot(qn, wukt_ref[i])
        else:
            latb = latent.astype(BF16)
            kf = _dot(latb, wk_ref[...])
            k_ref[0, rows, :] = (kf + jnp.concatenate([krot] * n_heads, axis=1)).astype(BF16)
            v_ref[0, rows, :] = _dot(latb, wv_ref[...]).astype(BF16)

    sub = min(tm, PROJ_ROWS)
    chunks = [pl.ds(r0, sub) for r0 in range(0, tm, sub)]
    products = [in_proj(rows) for rows in chunks]
    for rows, prod in zip(chunks, products):
        rest(rows, sub, prod)


def _const_spec(shape):
    nd = len(shape)
    return pl.BlockSpec(shape, lambda *_: (0,) * nd, pipeline_mode=pl.Buffered(1))


def _proj(dims, sample, tm, x, sh, sc, tabs, weights, extra):
    conv_dim, q_rank, kv_rank, n_heads, nope, half, _ = dims
    nb, s, d = x.shape
    r = sh.shape[1]
    grid = (nb, s // tm)
    row = lambda b, i: (b, i, 0)
    mod_spec = pl.BlockSpec((1, r, d), (lambda b, i: (b, 0, 0)))
    tab_spec = pl.BlockSpec((tm, LANES), lambda b, i: (i, 0))
    wmain, convw, gq, gkv, wq = weights
    in_specs = [pl.BlockSpec((1, tm, d), row), mod_spec, mod_spec, tab_spec, tab_spec, tab_spec,
                _const_spec(wmain.shape), _const_spec(convw.shape),
                _const_spec(gq.shape), _const_spec(gkv.shape), _const_spec(wq.shape)]
    in_specs += [_const_spec(e.shape) for e in extra]
    hq = n_heads * HEAD_PAD
    out_shape = [jax.ShapeDtypeStruct((nb, s, conv_dim), BF16)]
    out_specs = [pl.BlockSpec((1, tm, conv_dim), row)]
    if sample:
        out_shape += [jax.ShapeDtypeStruct((nb, s, hq), F32),
                      jax.ShapeDtypeStruct((nb, s, n_heads * kv_rank), F32)]
        out_specs += [pl.BlockSpec((1, tm, hq), row), pl.BlockSpec((1, tm, n_heads * kv_rank), row)]
    else:
        out_shape += [jax.ShapeDtypeStruct((nb, s, hq), BF16), jax.ShapeDtypeStruct((nb, s, hq), BF16),
                      jax.ShapeDtypeStruct((nb, s, n_heads * (hq // n_heads // 2)), BF16)]
        out_specs += [pl.BlockSpec((1, tm, hq), row), pl.BlockSpec((1, tm, hq), row),
                      pl.BlockSpec((1, tm, n_heads * (hq // n_heads // 2)), row)]
    out_shape += [jax.ShapeDtypeStruct((nb, s, kv_rank), F32), jax.ShapeDtypeStruct((nb, s, 2 * half), F32)]
    out_specs += [pl.BlockSpec((1, tm, kv_rank), row), pl.BlockSpec((1, tm, 2 * half), row)]
    scratch = []
    if sample:
        out_shape += [jax.ShapeDtypeStruct((nb, s, conv_dim), F32)]
        out_specs += [pl.BlockSpec((1, tm, conv_dim), row)]
    else:
        out_shape += [jax.ShapeDtypeStruct((nb, SUBLANES, conv_dim), F32)]
        out_specs += [pl.BlockSpec((1, SUBLANES, conv_dim), lambda b, i: (b, 0, 0))]
        scratch = [pltpu.VMEM((SUBLANES + tm, conv_dim), F32)]
    return pl.pallas_call(
        functools.partial(_proj_body, dims, sample),
        grid=grid, in_specs=in_specs, out_specs=out_specs, out_shape=out_shape,
        scratch_shapes=scratch,
        compiler_params=pltpu.CompilerParams(dimension_semantics=("arbitrary", "arbitrary"),
                                             vmem_limit_bytes=VMEM_LIMIT),
        name="proj_sample" if sample else "proj_prompt",
    )(x, sh, sc, *tabs, *weights, *extra)


def _attn_prompt_body(v_dim, q_ref, k_ref, v_ref, o_ref, vt_ref):
    tq = q_ref.shape[1]
    tk = tq
    qi = pl.program_id(2)
    n_h = q_ref.shape[2] // HEAD_PAD
    vrows = vt_ref.shape[1]

    @pl.when(qi == 0)
    def _():
        vt = v_ref[0].astype(F32).T.astype(BF16)
        tail = (lax.broadcasted_iota(jnp.int32, (vrows - v_dim, vt.shape[1]), 0) == 0).astype(BF16)
        for hh in range(n_h):
            vt_ref[hh, 0:v_dim, :] = vt[v_dim * hh:v_dim * (hh + 1), :]
            vt_ref[hh, v_dim:vrows, :] = tail

    keys = lax.broadcasted_iota(jnp.int32, (tk, tq), 0)
    queries = lax.broadcasted_iota(jnp.int32, (tk, tq), 1)
    qts = [q_ref[0, :, HEAD_PAD * hh:HEAD_PAD * (hh + 1)].astype(F32).T.astype(BF16) for hh in range(n_h)]

    def step(j, carry, masked):
        at = pl.ds(pl.multiple_of(j * tk, tk), tk)

        def scores(hh):
            s = _dot(k_ref[0, at, HEAD_PAD * hh:HEAD_PAD * (hh + 1)], qts[hh])
            return jnp.where(keys <= queries, s, NEG_INF) if masked else s

        new = []
        ahead = [scores(hh) for hh in range(min(SKEW, n_h))]
        for hh, (m, acc) in enumerate(carry):
            s = ahead.pop(0)
            if hh + SKEW < n_h:
                ahead.append(scores(hh + SKEW))
            m_new = jnp.maximum(m, jnp.max(s, axis=0, keepdims=True))
            p = jnp.exp2(s - m_new).astype(BF16)
            acc = jnp.exp2(m - m_new) * acc + _dot(vt_ref[hh, :, at], p)
            new.append((m_new, acc))
        return tuple(new)

    init = tuple((jnp.full((1, tq), NEG_INF, F32), jnp.zeros((vrows, tq), F32)) for _ in range(n_h))
    carry = lax.fori_loop(0, qi, functools.partial(step, masked=False), init)
    carry = step(qi, carry, True)
    ot = jnp.concatenate([acc[0:v_dim] / acc[v_dim:v_dim + 1] for _, acc in carry], axis=0)
    o_ref[0] = ot.T.astype(o_ref.dtype)


def _attn_prompt(q, k, v, n_heads, tq=512, hg=8):
    b, s, _ = q.shape
    v_dim = v.shape[2] // n_heads
    assert n_heads % hg == 0 and (hg * v_dim) % LANES == 0
    return pl.pallas_call(
        functools.partial(_attn_prompt_body, v_dim),
        grid=(b, n_heads // hg, s // tq),
        in_specs=[pl.BlockSpec((1, tq, hg * HEAD_PAD), lambda bi, hp, qi: (bi, qi, hp)),
                  pl.BlockSpec((1, s, hg * HEAD_PAD), lambda bi, hp, qi: (bi, 0, hp)),
                  pl.BlockSpec((1, s, hg * v_dim), lambda bi, hp, qi: (bi, 0, hp))],
        out_specs=pl.BlockSpec((1, tq, hg * v_dim), lambda bi, hp, qi: (bi, qi, hp)),
        out_shape=jax.ShapeDtypeStruct((b, s, n_heads * v_dim), BF16),
        scratch_shapes=[pltpu.VMEM((hg, v_dim + 2 * SUBLANES, s), BF16)],
        compiler_params=pltpu.CompilerParams(dimension_semantics=("arbitrary",) * 3,
                                             vmem_limit_bytes=VMEM_LIMIT),
        name="attn_prompt",
    )(q, k, v)


def _attn_sample_body(geom, pt_ref, q_ref, qlat_ref, latn_ref, krn_ref, lat_hbm, krt_hbm, o_ref,
                      lat_buf, kr_buf, kb_buf, s_buf, st_ref, acc_ref, p_ref, sems):
    nseq, n_pages, ch, page, nope, rope = geom
    n_chunks = n_pages // ch
    n_steps = nseq * n_chunks
    n_heads = o_ref.shape[1]
    hp = q_ref.shape[1]
    n_slots = lat_buf.shape[0]
    n_kb = kb_buf.shape[0]

    def page_copies(pg, slot, j):
        cols = pl.ds(j * page, page)
        return (pltpu.make_async_copy(lat_hbm.at[pg], lat_buf.at[slot, cols], sems.at[0, slot]),
                pltpu.make_async_copy(krt_hbm.at[pg], kr_buf.at[slot, j], sems.at[1, slot]))

    def start_chunk(g, slot):
        for j in range(ch):
            for cp in page_copies(pt_ref[g * ch + j], slot, j):
                cp.start()

    def wait_chunk(slot):
        pltpu.make_async_copy(lat_buf.at[slot], lat_buf.at[slot], sems.at[0, slot]).wait()
        pltpu.make_async_copy(kr_buf.at[slot], kr_buf.at[slot], sems.at[1, slot]).wait()

    def put(i, col):
        st_ref[i] = jnp.broadcast_to(col, st_ref.shape[1:])

    def substep(a, u):
        m, l, alpha = st_ref[0][:, 0:1], st_ref[1][:, 0:1], st_ref[2][:, 0:1]
        acc, p = acc_ref[...], p_ref[...]
        wait_chunk(u)

        acc = alpha * acc + _dot(p, kb_buf[(u - 2) % n_kb])
        out_row = jnp.minimum(lax.div(jnp.maximum(a - 2, 0), n_chunks), nseq)
        o_ref[out_row] = (acc / l)[0:n_heads, :]

        seq_a = jnp.minimum(lax.div(a, n_chunks), nseq - 1)
        refill = jnp.minimum(a + n_slots - 1, n_steps - 1) * ch
        rslot = (u + n_slots - 1) % n_slots
        pages = []
        for j in range(ch):
            rows = pl.ds(j * page, page)
            kbj = lat_buf[u, rows, :].astype(BF16)
            kb_buf[u % n_kb, rows, :] = kbj
            pages.append(kbj)
            for cp in page_copies(pt_ref[refill + j], rslot, j):
                cp.start()
        kb = jnp.concatenate(pages, axis=0)
        krb = jnp.concatenate([kr_buf[u, j] for j in range(ch)], axis=1).astype(BF16)
        qa = q_ref[seq_a]
        s_buf[u % 2] = (_dot_nt(qlat_ref[seq_a].astype(BF16), kb)
                        + _dot(qa[:, nope:nope + rope].astype(BF16), krb))

        b = jnp.maximum(a - 1, 0)
        seq_b = jnp.minimum(lax.div(b, n_chunks), nseq - 1)
        first = lax.rem(b, n_chunks) == 0
        qb = q_ref[seq_b]
        qlat = qlat_ref[seq_b]
        latn = latn_ref[pl.ds(seq_b, 1), :]
        krn = krn_ref[pl.ds(seq_b, 1), :]
        s_new = (jnp.sum(qlat * latn, axis=-1, keepdims=True)
                 + jnp.sum(qb[:, nope:nope + rope] * krn, axis=-1, keepdims=True))
        m = jnp.where(first, s_new, m)
        l = jnp.where(first, 1.0, l)
        acc = jnp.where(first, jnp.broadcast_to(latn, acc.shape), acc)
        s = s_buf[(u - 1) % 2]
        m_new = jnp.maximum(m, jnp.max(s, axis=-1, keepdims=True))
        p = jnp.exp2(s - m_new)
        alpha = jnp.exp2(m - m_new)
        put(0, m_new)
        put(1, alpha * l + jnp.sum(p, axis=-1, keepdims=True))
        put(2, alpha)
        acc_ref[...] = acc
        p_ref[...] = p.astype(BF16)

    i = pl.program_id(0)

    @pl.when(i == 0)
    def _():
        for c in range(n_slots - 1):
            start_chunk(i + c, c)
        for k in range(2, n_kb):
            kb_buf[k] = jnp.zeros(kb_buf.shape[1:], BF16)
        s_buf[1] = jnp.zeros(s_buf.shape[1:], F32)
        p_ref[...] = jnp.zeros(p_ref.shape, BF16)
        acc_ref[...] = jnp.zeros(acc_ref.shape, F32)
        put(0, jnp.zeros((hp, 1), F32))
        put(1, jnp.ones((hp, 1), F32))
        put(2, jnp.zeros((hp, 1), F32))

    for u in range(n_slots):
        pl.when(i >= 0)(functools.partial(substep, n_slots * i + u, u))

    @pl.when(i == pl.num_programs(0) - 1)
    def _():
        for slot in range(n_slots - 1):
            wait_chunk(slot)


def _attn_sample(page_table, q16, qlat16, lat_new, kr_new, cache_lat, cache_krt, n_heads, nope,
                 ch=32, n_slots=4):
    nseq, n_pages = page_table.shape
    _, page, kv_rank = cache_lat.shape
    rope = cache_krt.shape[1]
    assert n_pages % ch == 0 and n_slots % 4 == 0
    chk = ch * page
    hp = q16.shape[1]
    geom = (nseq, n_pages, ch, page, nope, rope)
    positions = nseq * (n_pages // ch) + 2
    whole = lambda a: pl.BlockSpec(a.shape, lambda i, pt: (0,) * a.ndim)
    grid_spec = pltpu.PrefetchScalarGridSpec(
        num_scalar_prefetch=1,
        grid=(pl.cdiv(positions, n_slots),),
        in_specs=[whole(q16), whole(qlat16), whole(lat_new), whole(kr_new),
                  pl.BlockSpec(memory_space=pl.ANY), pl.BlockSpec(memory_space=pl.ANY)],
        out_specs=pl.BlockSpec((nseq + 1, n_heads, kv_rank), lambda i, pt: (0, 0, 0)),
        scratch_shapes=[pltpu.VMEM((n_slots, chk, kv_rank), F32),
                        pltpu.VMEM((n_slots, ch, rope, page), F32),
                        pltpu.VMEM((4, chk, kv_rank), BF16),
                        pltpu.VMEM((2, hp, chk), F32),
                        pltpu.VMEM((3, hp, LANES), F32),
                        pltpu.VMEM((hp, kv_rank), F32),
                        pltpu.VMEM((hp, chk), BF16),
                        pltpu.SemaphoreType.DMA((2, n_slots))],
    )
    out = pl.pallas_call(
        functools.partial(_attn_sample_body, geom),
        grid_spec=grid_spec,
        out_shape=jax.ShapeDtypeStruct((nseq + 1, n_heads, kv_rank), F32),
        compiler_params=pltpu.CompilerParams(dimension_semantics=("arbitrary",),
                                             vmem_limit_bytes=VMEM_LIMIT),
        name="attn_sample",
    )(page_table.reshape(-1), q16, qlat16, lat_new, kr_new, cache_lat, cache_krt)
    return out[:nseq]


def _post_body(alpha, ff_chunks, sample, *refs):
    if sample:
        (x_ref, convout_ref, olat_ref, ga_ref, shf_ref, scf_ref, gf_ref, woc_ref, woa_ref,
         ln1g_ref, ln1b_ref, wg_ref, wu_ref, wd_ref, ln2g_ref, ln2b_ref, wuv_ref, y_ref) = refs
    else:
        (x_ref, convout_ref, attn_ref, ga_ref, shf_ref, scf_ref, gf_ref, woc_ref, woa_ref,
         ln1g_ref, ln1b_ref, wg_ref, wu_ref, wd_ref, ln2g_ref, ln2b_ref, y_ref) = refs
    tm = x_ref.shape[1]

    def mod(ref, rows):
        return ref[0] if ref.shape[1] == 1 else ref[0, rows, :]

    sub = min(tm, POST_ROWS)
    chunks = [pl.ds(r0, sub) for r0 in range(0, tm, sub)]

    def out_proj(rows):
        if sample:
            attn = _dot(olat_ref[0, rows, :].astype(BF16), wuv_ref[...]).astype(BF16)
        else:
            attn = attn_ref[0, rows, :]
        return _dot(convout_ref[0, rows, :], woc_ref[...]) + _dot(attn, woa_ref[...])

    def norm1(rows, a):
        return _layer_norm(alpha * x_ref[0, rows, :] + (1.0 + mod(ga_ref, rows)) * a, ln1g_ref[...], ln1b_ref[...])

    def ffn(rows, x1):
        ub = (x1 * (1.0 + mod(scf_ref, rows)) + mod(shf_ref, rows)).astype(BF16)
        gu = [(_dot(ub, wg_ref[:, lo:hi]), _dot(ub, wu_ref[:, lo:hi])) for lo, hi in ff_chunks]
        f = None
        for (lo, hi), (g, up) in zip(ff_chunks, gu):
            hmid = (g * jax.nn.sigmoid(g) * up).astype(BF16)
            part = _dot(hmid, wd_ref[lo:hi, :])
            f = part if f is None else f + part
        return f

    def norm2(rows, x1, f):
        y_ref[0, rows, :] = _layer_norm(alpha * x1 + (1.0 + mod(gf_ref, rows)) * f, ln2g_ref[...], ln2b_ref[...])

    a = [out_proj(r) for r in chunks]
    x1 = [norm1(r, ai) for r, ai in zip(chunks, a)]
    f = [None] * len(chunks)
    for i, r in enumerate(chunks):
        f[i] = ffn(r, x1[i])
        if i > 0:
            norm2(chunks[i - 1], x1[i - 1], f[i - 1])
    norm2(chunks[-1], x1[-1], f[-1])


def _post(alpha, sample, tm, x, convout, attn, mods, weights, extra):
    nb, s, d = x.shape
    r = mods[0].shape[1]
    d_ff = weights[5].shape[1]
    cut = min(d_ff, pl.cdiv(d_ff // 2, MXU_TILE) * MXU_TILE)
    ff_chunks = tuple(c for c in ((0, cut), (cut, d_ff)) if c[1] > c[0])
    row = lambda b, i: (b, i, 0)
    mod_spec = pl.BlockSpec((1, r, d), lambda b, i: (b, 0, 0))
    in_specs = [pl.BlockSpec((1, tm, d), row),
                pl.BlockSpec((1, tm, convout.shape[2]), row),
                pl.BlockSpec((1, tm, attn.shape[2]), row),
                mod_spec, mod_spec, mod_spec, mod_spec]
    in_specs += [_const_spec(w.shape) for w in weights]
    in_specs += [_const_spec(e.shape) for e in extra]
    return pl.pallas_call(
        functools.partial(_post_body, alpha, ff_chunks, sample),
        grid=(nb, s // tm), in_specs=in_specs,
        out_specs=pl.BlockSpec((1, tm, d), row),
        out_shape=jax.ShapeDtypeStruct((nb, s, d), F32),
        compiler_params=pltpu.CompilerParams(dimension_semantics=("arbitrary", "arbitrary"),
                                             vmem_limit_bytes=VMEM_LIMIT),
        name="post_sample" if sample else "post_prompt",
    )(x, convout, attn, *mods, *weights, *extra)


def _rope_tables(pos, rope, nope):
    half = rope // 2
    inv = 1.0 / (ROPE_THETA ** (jnp.arange(0, rope, 2, dtype=F32) / rope))
    ang = pos.astype(F32)[:, None] * inv[None, :]
    cos, sin = jnp.cos(ang), jnp.sin(ang)
    n = pos.shape[0]
    z = lambda w: jnp.zeros((n, w), F32)
    pad = LANES - nope - rope
    ta = jnp.concatenate([jnp.ones((n, nope), F32), cos, cos, z(pad)], axis=1)
    tb = jnp.concatenate([z(nope + half), sin, z(pad)], axis=1)
    tc = jnp.concatenate([z(nope), -sin, z(half + pad)], axis=1)
    return ta, tb, tc


def kernel(x_prompt, x_sample, cache_latent, cache_k_rope, state_conv, page_table, c_prompt, c_sample,
           w_ada, b_ada, w_in, conv_w, g_q, g_kv, w_uq, w_uk, w_uv, w_o, ln1_g, ln1_b, w_gate, w_up,
           w_down, ln2_g, ln2_b):
    depth = w_ada.shape[0]
    nb, seq, d = x_prompt.shape
    ns, dec_seq, _ = x_sample.shape
    assert dec_seq == 1
    conv_dim = conv_w.shape[2]
    q_rank = g_q.shape[1]
    kv_rank = g_kv.shape[1]
    n_heads, nope = w_uk.shape[2], w_uk.shape[3]
    rope = w_uq.shape[3] - nope
    v_dim = w_uv.shape[3]
    half = rope // 2
    page = cache_latent.shape[2]
    past_len = page_table.shape[1] * page
    alpha = (2 * depth) ** 0.25
    scale = (nope + rope) ** -0.5 * LOG2E
    dims = (conv_dim, q_rank, kv_rank, n_heads, nope, half, scale)
    pad = HEAD_PAD - nope - rope
    assert pad >= 0 and 2 * v_dim == HEAD_PAD

    tabs_p = _rope_tables(jnp.arange(seq), rope, nope)
    tabs_s = _rope_tables(jnp.full((ns,), past_len), rope, nope)

    xp = x_prompt
    xs = x_sample.reshape(1, ns, d)
    c_all = jnp.concatenate([c_prompt, c_sample], axis=0)
    c_all = jnp.pad(c_all, ((0, -(nb + ns) % (2 * SUBLANES)), (0, 0)))
    outs = [[] for _ in range(6)]
    for l in range(depth):
        w_in_l = w_in[l]
        c4 = 3 * conv_dim + q_rank
        c5 = c4 + kv_rank
        wkr = jnp.pad(w_in_l[:, c5:], ((0, 0), (nope, pad)))
        wmain = jnp.concatenate([w_in_l[:, :c4], wkr, w_in_l[:, c4:c5]], axis=1).astype(BF16)
        wq = jnp.pad(w_uq[l], ((0, 0), (0, 0), (0, pad))).reshape(q_rank, n_heads * HEAD_PAD).astype(BF16)
        wk = jnp.pad(w_uk[l], ((0, 0), (0, 0), (0, HEAD_PAD - nope))).reshape(kv_rank, n_heads * HEAD_PAD)
        wk = wk.astype(BF16)
        wv = w_uv[l].reshape(kv_rank, n_heads * v_dim).astype(BF16)
        wukt = jnp.transpose(w_uk[l], (1, 2, 0)).astype(BF16)
        eye = jnp.eye(n_heads, dtype=F32)
        wuv_bd = (w_uv[l].transpose(1, 0, 2)[:, :, None, :] * eye[:, None, :, None])
        wuv_bd = wuv_bd.reshape(n_heads * kv_rank, n_heads * v_dim).astype(BF16)
        woc = w_o[l, :conv_dim].astype(BF16)
        woa = w_o[l, conv_dim:].astype(BF16)
        proj_w = (wmain, conv_w[l], g_q[l].reshape(1, -1), g_kv[l].reshape(1, -1), wq)
        post_w = (woc, woa, ln1_g[l].reshape(1, -1), ln1_b[l].reshape(1, -1), w_gate[l].astype(BF16),
                  w_up[l].astype(BF16), w_down[l].astype(BF16), ln2_g[l].reshape(1, -1),
                  ln2_b[l].reshape(1, -1))

        mod = _ada(c_all, w_ada[l], b_ada[l])
        mp = [mod[:nb, i * d:(i + 1) * d].reshape(nb, 1, d) for i in range(6)]
        ms = [mod[nb:nb + ns, i * d:(i + 1) * d].reshape(1, ns, d) for i in range(6)]

        convout, q, k, v, lat_p, kr_p, tail = _proj(dims, False, 1024, xp, mp[0], mp[1], tabs_p, proj_w,
                                                     (wk, wv))
        attn = _attn_prompt(q, k, v, n_heads)
        xp = _post(alpha, False, 1024, xp, convout, attn, (mp[2], mp[3], mp[4], mp[5]), post_w, ())

        st = state_conv[l]
        convout_s, q_s, qlat_s, lat_s, kr_s, cin_s = _proj(
            dims, True, ns, xs, ms[0], ms[1], tabs_s, proj_w, (st[:, 0], st[:, 1], wukt))
        rows_pad = ((0, 0), (0, 2 * SUBLANES - n_heads), (0, 0))
        q16 = jnp.pad(q_s.reshape(ns, n_heads, HEAD_PAD), rows_pad)
        qlat16 = jnp.pad(qlat_s.reshape(ns, n_heads, kv_rank), rows_pad)
        cache_krt = jnp.swapaxes(cache_k_rope[l], 1, 2)
        olat = _attn_sample(page_table, q16, qlat16, lat_s[0], kr_s[0], cache_latent[l], cache_krt,
                            n_heads, nope)
        xs = _post(alpha, True, ns, xs, convout_s, olat.reshape(1, ns, n_heads * kv_rank),
                   (ms[2], ms[3], ms[4], ms[5]), post_w, (wuv_bd,))

        outs[0].append(lat_p)
        outs[1].append(kr_p)
        outs[2].append(tail[:, SUBLANES - 2:, :])
        outs[3].append(lat_s.reshape(ns, 1, kv_rank))
        outs[4].append(kr_s.reshape(ns, 1, rope))
        outs[5].append(jnp.stack([st[:, 1], cin_s[0]], axis=1))
    return (xp, xs.reshape(ns, 1, d), jnp.stack(outs[0]), jnp.stack(outs[1]), jnp.stack(outs[2]),
            jnp.stack(outs[3]), jnp.stack(outs[4]), jnp.stack(outs[5]))
```

```python
import functools
import math

import jax
import jax.numpy as jnp
from jax import lax
from jax.experimental import pallas as pl
from jax.experimental.pallas import tpu as pltpu

F32 = jnp.float32
BF16 = jnp.bfloat16

ROPE_THETA = 10000.0
LN_EPS = 1e-5
RMS_EPS = 1e-6
NEG_INF = -1e30
LOG2E = math.log2(math.e)

LANES = 128
SUBLANES = 8
MXU_TILE = 256
PROJ_ROWS = 512
POST_ROWS = 512
SKEW = 2
HEAD_PAD = 128
VMEM_LIMIT = 56 * 1024 * 1024


def _dot(a, b):
    return jnp.dot(a, b, preferred_element_type=F32)


def _dot_nt(a, b):
    return lax.dot_general(a, b, (((1,), (1,)), ((), ())), preferred_element_type=F32)


def _layer_norm(r, g, b):
    mu = jnp.mean(r, axis=-1, keepdims=True)
    d = r - mu
    var = jnp.mean(d * d, axis=-1, keepdims=True)
    return d * lax.rsqrt(var + LN_EPS) * g + b


def _rms_norm(x, g):
    return x * lax.rsqrt(jnp.mean(x * x, axis=-1, keepdims=True) + RMS_EPS) * g


def _rope_group(x, ta, tb, tc, half):
    return x * ta + pltpu.roll(x, half, 1) * tb + pltpu.roll(x, LANES - half, 1) * tc


def _ada_body(c_ref, w_ref, b_ref, o_ref):
    c = c_ref[...]
    s = c * jax.nn.sigmoid(c)
    o_ref[...] = _dot(s.astype(BF16), w_ref[...].astype(BF16)) + b_ref[...]


def _ada(c_all, w_ada, b_ada, tn=1024):
    m, d = c_all.shape
    n = w_ada.shape[1]
    return pl.pallas_call(
        _ada_body,
        grid=(n // tn,),
        in_specs=[pl.BlockSpec((m, d), lambda j: (0, 0)),
                  pl.BlockSpec((d, tn), lambda j: (0, j)),
                  pl.BlockSpec((1, tn), lambda j: (0, j))],
        out_specs=pl.BlockSpec((m, tn), lambda j: (0, j)),
        out_shape=jax.ShapeDtypeStruct((m, n), F32),
        compiler_params=pltpu.CompilerParams(dimension_semantics=("arbitrary",),
                                             vmem_limit_bytes=VMEM_LIMIT),
        name="ada",
    )(c_all, w_ada, b_ada.reshape(1, n))


def _proj_body(dims, sample, *refs):
    conv_dim, q_rank, kv_rank, n_heads, nope, half, scale = dims
    if sample:
        (x_ref, sh_ref, sc_ref, ta_ref, tb_ref, tc_ref, wmain_ref, convw_ref, gq_ref, gkv_ref,
         wq_ref, s0_ref, s1_ref, wukt_ref,
         convout_ref, q_ref, qlat_ref, lat_ref, krot_ref, cin_ref) = refs
    else:
        (x_ref, sh_ref, sc_ref, ta_ref, tb_ref, tc_ref, wmain_ref, convw_ref, gq_ref, gkv_ref,
         wq_ref, wk_ref, wv_ref,
         convout_ref, q_ref, k_ref, v_ref, lat_ref, krot_ref, tail_ref, cin_buf) = refs
    tm = x_ref.shape[1]
    c1, c2, c3 = conv_dim, 2 * conv_dim, 3 * conv_dim
    c4 = c3 + q_rank + LANES
    c5 = c4 + kv_rank
    w0 = convw_ref[0:1, :]
    w1 = convw_ref[1:2, :]
    w2 = convw_ref[2:3, :]
    if not sample:
        @pl.when(pl.program_id(1) == 0)
        def _():
            cin_buf[0:SUBLANES, :] = jnp.zeros((SUBLANES, conv_dim), F32)

    def in_proj(rows):
        x = x_ref[0, rows, :]
        sc = sc_ref[0] if sc_ref.shape[1] == 1 else sc_ref[0, rows, :]
        sh = sh_ref[0] if sh_ref.shape[1] == 1 else sh_ref[0, rows, :]
        ub = (x * (1.0 + sc) + sh).astype(BF16)
        return (_dot(ub, wmain_ref[:, 0:c1]), _dot(ub, wmain_ref[:, c1:c2]), _dot(ub, wmain_ref[:, c2:c3]),
                _dot(ub, wmain_ref[:, c3:c4]), _dot(ub, wmain_ref[:, c4:c5]))

    def rest(rows, sub, products):
        h, gb, gc, cq_kr, ckv = products
        cq = cq_kr[:, 0:q_rank]
        kr = cq_kr[:, q_rank:q_rank + LANES]

        conv_in = gc * h
        if sample:
            y = w0 * s0_ref[rows, :] + w1 * s1_ref[rows, :] + w2 * conv_in
            cin_ref[0, rows, :] = conv_in
        else:
            cin_buf[SUBLANES:SUBLANES + sub, :] = conv_in
            y = (w0 * cin_buf[SUBLANES - 2:SUBLANES - 2 + sub, :]
                 + w1 * cin_buf[SUBLANES - 1:SUBLANES - 1 + sub, :] + w2 * conv_in)
            tail = conv_in[sub - SUBLANES:sub, :]
            cin_buf[0:SUBLANES, :] = tail
            tail_ref[0] = tail
        convout_ref[0, rows, :] = (gb * y).astype(BF16)

        ta = ta_ref[rows, :]
        tb = tb_ref[rows, :]
        tc = tc_ref[rows, :]
        cqn = _rms_norm(cq, gq_ref[...]).astype(BF16)
        qf = _dot(cqn, wq_ref[...])
        q_heads = [_rope_group(qf[:, HEAD_PAD * i:HEAD_PAD * (i + 1)], ta, tb, tc, half) * scale
                   for i in range(n_heads)]
        q_ref[0, rows, :] = jnp.concatenate(q_heads, axis=1).astype(q_ref.dtype)

        latent = _rms_norm(ckv, gkv_ref[...])
        lat_ref[0, rows, :] = latent
        krot = _rope_group(kr, ta, tb, tc, half)
        krot_ref[0, rows, :] = krot[:, nope:nope + 2 * half]

        if sample:
            for i in range(n_heads):
                qn = q_heads[i][:, 0:nope].astype(BF16)
                qlat_ref[0, rows, kv_rank * i:kv_rank * (i + 1)] = _dot(qn, wukt_ref[i])
        else:
            latb = latent.astype(BF16)
            kf = _dot(latb, wk_ref[...])
            k_ref[0, rows, :] = (kf + jnp.concatenate([krot] * n_heads, axis=1)).astype(BF16)
            v_ref[0, rows, :] = _dot(latb, wv_ref[...]).astype(BF16)

    sub = min(tm, PROJ_ROWS)
    chunks = [pl.ds(r0, sub) for r0 in range(0, tm, sub)]
    products = [in_proj(rows) for rows in chunks]
    for rows, prod in zip(chunks, products):
        rest(rows, sub, prod)


def _const_spec(shape):
    nd = len(shape)
    return pl.BlockSpec(shape, lambda *_: (0,) * nd, pipeline_mode=pl.Buffered(1))


def _mod_spec(mod, d, col):
    return pl.BlockSpec((1, mod.shape[1], d), lambda b, i: (b, 0, col))


def _proj(dims, sample, tm, x, mod, tabs, weights, extra):
    conv_dim, q_rank, kv_rank, n_heads, nope, half, _ = dims
    nb, s, d = x.shape
    grid = (nb, s // tm)
    row = lambda b, i: (b, i, 0)
    tab_spec = pl.BlockSpec((tm, LANES), lambda b, i: (i, 0))
    wmain, convw, gq, gkv, wq = weights
    in_specs = [pl.BlockSpec((1, tm, d), row), _mod_spec(mod, d, 0), _mod_spec(mod, d, 1),
                tab_spec, tab_spec, tab_spec,
                _const_spec(wmain.shape), _const_spec(convw.shape),
                _const_spec(gq.shape), _const_spec(gkv.shape), _const_spec(wq.shape)]
    in_specs += [_const_spec(e.shape) for e in extra]
    hq = n_heads * HEAD_PAD
    out_shape = [jax.ShapeDtypeStruct((nb, s, conv_dim), BF16)]
    out_specs = [pl.BlockSpec((1, tm, conv_dim), row)]
    if sample:
        out_shape += [jax.ShapeDtypeStruct((nb, s, hq), F32),
                      jax.ShapeDtypeStruct((nb, s, n_heads * kv_rank), F32)]
        out_specs += [pl.BlockSpec((1, tm, hq), row), pl.BlockSpec((1, tm, n_heads * kv_rank), row)]
    else:
        out_shape += [jax.ShapeDtypeStruct((nb, s, hq), BF16), jax.ShapeDtypeStruct((nb, s, hq), BF16),
                      jax.ShapeDtypeStruct((nb, s, n_heads * (hq // n_heads // 2)), BF16)]
        out_specs += [pl.BlockSpec((1, tm, hq), row), pl.BlockSpec((1, tm, hq), row),
                      pl.BlockSpec((1, tm, n_heads * (hq // n_heads // 2)), row)]
    out_shape += [jax.ShapeDtypeStruct((nb, s, kv_rank), F32), jax.ShapeDtypeStruct((nb, s, 2 * half), F32)]
    out_specs += [pl.BlockSpec((1, tm, kv_rank), row), pl.BlockSpec((1, tm, 2 * half), row)]
    scratch = []
    if sample:
        out_shape += [jax.ShapeDtypeStruct((nb, s, conv_dim), F32)]
        out_specs += [pl.BlockSpec((1, tm, conv_dim), row)]
    else:
        out_shape += [jax.ShapeDtypeStruct((nb, SUBLANES, conv_dim), F32)]
        out_specs += [pl.BlockSpec((1, SUBLANES, conv_dim), lambda b, i: (b, 0, 0))]
        scratch = [pltpu.VMEM((SUBLANES + tm, conv_dim), F32)]
    return pl.pallas_call(
        functools.partial(_proj_body, dims, sample),
        grid=grid, in_specs=in_specs, out_specs=out_specs, out_shape=out_shape,
        scratch_shapes=scratch,
        compiler_params=pltpu.CompilerParams(dimension_semantics=("arbitrary", "arbitrary"),
                                             vmem_limit_bytes=VMEM_LIMIT),
        name="proj_sample" if sample else "proj_prompt",
    )(x, mod, mod, *tabs, *weights, *extra)


def _attn_prompt_body(v_dim, q_ref, k_ref, v_ref, o_ref, vt_ref):
    tq = q_ref.shape[1]
    tk = tq
    qi = pl.program_id(2)
    n_h = q_ref.shape[2] // HEAD_PAD
    vrows = vt_ref.shape[1]

    @pl.when(qi == 0)
    def _():
        vt = v_ref[0].astype(F32).T.astype(BF16)
        tail = (lax.broadcasted_iota(jnp.int32, (vrows - v_dim, vt.shape[1]), 0) == 0).astype(BF16)
        for hh in range(n_h):
            vt_ref[hh, 0:v_dim, :] = vt[v_dim * hh:v_dim * (hh + 1), :]
            vt_ref[hh, v_dim:vrows, :] = tail

    keys = lax.broadcasted_iota(jnp.int32, (tk, tq), 0)
    queries = lax.broadcasted_iota(jnp.int32, (tk, tq), 1)
    qts = [q_ref[0, :, HEAD_PAD * hh:HEAD_PAD * (hh + 1)].astype(F32).T.astype(BF16) for hh in range(n_h)]

    def step(j, carry, masked):
        at = pl.ds(pl.multiple_of(j * tk, tk), tk)

        def scores(hh):
            s = _dot(k_ref[0, at, HEAD_PAD * hh:HEAD_PAD * (hh + 1)], qts[hh])
            return jnp.where(keys <= queries, s, NEG_INF) if masked else s

        new = []
        ahead = [scores(hh) for hh in range(min(SKEW, n_h))]
        for hh, (m, acc) in enumerate(carry):
            s = ahead.pop(0)
            if hh + SKEW < n_h:
                ahead.append(scores(hh + SKEW))
            m_new = jnp.maximum(m, jnp.max(s, axis=0, keepdims=True))
            p = jnp.exp2(s - m_new).astype(BF16)
            acc = jnp.exp2(m - m_new) * acc + _dot(vt_ref[hh, :, at], p)
            new.append((m_new, acc))
        return tuple(new)

    init = tuple((jnp.full((1, tq), NEG_INF, F32), jnp.zeros((vrows, tq), F32)) for _ in range(n_h))
    carry = lax.fori_loop(0, qi, functools.partial(step, masked=False), init)
    carry = step(qi, carry, True)
    ot = jnp.concatenate([acc[0:v_dim] / acc[v_dim:v_dim + 1] for _, acc in carry], axis=0)
    o_ref[0] = ot.T.astype(o_ref.dtype)


def _attn_prompt(q, k, v, n_heads, tq=512, hg=8):
    b, s, _ = q.shape
    v_dim = v.shape[2] // n_heads
    assert n_heads % hg == 0 and (hg * v_dim) % LANES == 0
    return pl.pallas_call(
        functools.partial(_attn_prompt_body, v_dim),
        grid=(b, n_heads // hg, s // tq),
        in_specs=[pl.BlockSpec((1, tq, hg * HEAD_PAD), lambda bi, hp, qi: (bi, qi, hp)),
                  pl.BlockSpec((1, s, hg * HEAD_PAD), lambda bi, hp, qi: (bi, 0, hp)),
                  pl.BlockSpec((1, s, hg * v_dim), lambda bi, hp, qi: (bi, 0, hp))],
        out_specs=pl.BlockSpec((1, tq, hg * v_dim), lambda bi, hp, qi: (bi, qi, hp)),
        out_shape=jax.ShapeDtypeStruct((b, s, n_heads * v_dim), BF16),
        scratch_shapes=[pltpu.VMEM((hg, v_dim + 2 * SUBLANES, s), BF16)],
        compiler_params=pltpu.CompilerParams(dimension_semantics=("arbitrary",) * 3,
                                             vmem_limit_bytes=VMEM_LIMIT),
        name="attn_prompt",
    )(q, k, v)


def _attn_sample_body(geom, pt_ref, q_ref, qlat_ref, latn_ref, krn_ref, lat_hbm, krt_hbm, o_ref,
                      lat_buf, kr_buf, kb_buf, s_buf, st_ref, acc_ref, p_ref, sems):
    nseq, n_pages, ch, page, nope, rope = geom
    n_chunks = n_pages // ch
    n_steps = nseq * n_chunks
    n_heads = o_ref.shape[1]
    hp = q_ref.shape[1]
    n_slots = lat_buf.shape[0]
    n_kb = kb_buf.shape[0]

    def page_copies(pg, slot, j):
        cols = pl.ds(j * page, page)
        return (pltpu.make_async_copy(lat_hbm.at[pg], lat_buf.at[slot, cols], sems.at[0, slot]),
                pltpu.make_async_copy(krt_hbm.at[pg], kr_buf.at[slot, j], sems.at[1, slot]))

    def start_chunk(g, slot):
        for j in range(ch):
            for cp in page_copies(pt_ref[g * ch + j], slot, j):
                cp.start()

    def wait_chunk(slot):
        pltpu.make_async_copy(lat_buf.at[slot], lat_buf.at[slot], sems.at[0, slot]).wait()
        pltpu.make_async_copy(kr_buf.at[slot], kr_buf.at[slot], sems.at[1, slot]).wait()

    def put(i, col):
        st_ref[i] = jnp.broadcast_to(col, st_ref.shape[1:])

    def substep(a, u):
        m, l, alpha = st_ref[0][:, 0:1], st_ref[1][:, 0:1], st_ref[2][:, 0:1]
        acc, p = acc_ref[...], p_ref[...]
        wait_chunk(u)

        acc = alpha * acc + _dot(p, kb_buf[(u - 2) % n_kb])
        out_row = jnp.minimum(lax.div(jnp.maximum(a - 2, 0), n_chunks), nseq)
        o_ref[out_row] = (acc / l)[0:n_heads, :]

        seq_a = jnp.minimum(lax.div(a, n_chunks), nseq - 1)
        refill = jnp.minimum(a + n_slots - 1, n_steps - 1) * ch
        rslot = (u + n_slots - 1) % n_slots
        pages = []
        for j in range(ch):
            rows = pl.ds(j * page, page)
            kbj = lat_buf[u, rows, :].astype(BF16)
            kb_buf[u % n_kb, rows, :] = kbj
            pages.append(kbj)
            for cp in page_copies(pt_ref[refill + j], rslot, j):
                cp.start()
        kb = jnp.concatenate(pages, axis=0)
        krb = jnp.concatenate([kr_buf[u, j] for j in range(ch)], axis=1).astype(BF16)
        qa = q_ref[seq_a]
        s_buf[u % 2] = (_dot_nt(qlat_ref[seq_a].astype(BF16), kb)
                        + _dot(qa[:, nope:nope + rope].astype(BF16), krb))

        b = jnp.maximum(a - 1, 0)
        seq_b = jnp.minimum(lax.div(b, n_chunks), nseq - 1)
        first = lax.rem(b, n_chunks) == 0
        qb = q_ref[seq_b]
        qlat = qlat_ref[seq_b]
        latn = latn_ref[pl.ds(seq_b, 1), :]
        krn = krn_ref[pl.ds(seq_b, 1), :]
        s_new = (jnp.sum(qlat * latn, axis=-1, keepdims=True)
                 + jnp.sum(qb[:, nope:nope + rope] * krn, axis=-1, keepdims=True))
        m = jnp.where(first, s_new, m)
        l = jnp.where(first, 1.0, l)
        acc = jnp.where(first, jnp.broadcast_to(latn, acc.shape), acc)
        s = s_buf[(u - 1) % 2]
        m_new = jnp.maximum(m, jnp.max(s, axis=-1, keepdims=True))
        p = jnp.exp2(s - m_new)
        alpha = jnp.exp2(m - m_new)
        put(0, m_new)
        put(1, alpha * l + jnp.sum(p, axis=-1, keepdims=True))
        put(2, alpha)
        acc_ref[...] = acc
        p_ref[...] = p.astype(BF16)

    i = pl.program_id(0)

    @pl.when(i == 0)
    def _():
        for c in range(n_slots - 1):
            start_chunk(i + c, c)
        for k in range(2, n_kb):
            kb_buf[k] = jnp.zeros(kb_buf.shape[1:], BF16)
        s_buf[1] = jnp.zeros(s_buf.shape[1:], F32)
        p_ref[...] = jnp.zeros(p_ref.shape, BF16)
        acc_ref[...] = jnp.zeros(acc_ref.shape, F32)
        put(0, jnp.zeros((hp, 1), F32))
        put(1, jnp.ones((hp, 1), F32))
        put(2, jnp.zeros((hp, 1), F32))

    for u in range(n_slots):
        pl.when(i >= 0)(functools.partial(substep, n_slots * i + u, u))

    @pl.when(i == pl.num_programs(0) - 1)
    def _():
        for slot in range(n_slots - 1):
            wait_chunk(slot)


def _attn_sample(page_table, q16, qlat16, lat_new, kr_new, cache_lat, cache_krt, n_heads, nope,
                 ch=32, n_slots=4):
    nseq, n_pages = page_table.shape
    _, page, kv_rank = cache_lat.shape
    rope = cache_krt.shape[1]
    assert n_pages % ch == 0 and n_slots % 4 == 0
    chk = ch * page
    hp = q16.shape[1]
    geom = (nseq, n_pages, ch, page, nope, rope)
    positions = nseq * (n_pages // ch) + 2
    whole = lambda a: pl.BlockSpec(a.shape, lambda i, pt: (0,) * a.ndim)
    grid_spec = pltpu.PrefetchScalarGridSpec(
        num_scalar_prefetch=1,
        grid=(pl.cdiv(positions, n_slots),),
        in_specs=[whole(q16), whole(qlat16), whole(lat_new), whole(kr_new),
                  pl.BlockSpec(memory_space=pl.ANY), pl.BlockSpec(memory_space=pl.ANY)],
        out_specs=pl.BlockSpec((nseq + 1, n_heads, kv_rank), lambda i, pt: (0, 0, 0)),
        scratch_shapes=[pltpu.VMEM((n_slots, chk, kv_rank), F32),
                        pltpu.VMEM((n_slots, ch, rope, page), F32),
                        pltpu.VMEM((4, chk, kv_rank), BF16),
                        pltpu.VMEM((2, hp, chk), F32),
                        pltpu.VMEM((3, hp, LANES), F32),
                        pltpu.VMEM((hp, kv_rank), F32),
                        pltpu.VMEM((hp, chk), BF16),
                        pltpu.SemaphoreType.DMA((2, n_slots))],
    )
    out = pl.pallas_call(
        functools.partial(_attn_sample_body, geom),
        grid_spec=grid_spec,
        out_shape=jax.ShapeDtypeStruct((nseq + 1, n_heads, kv_rank), F32),
        compiler_params=pltpu.CompilerParams(dimension_semantics=("arbitrary",),
                                             vmem_limit_bytes=VMEM_LIMIT),
        name="attn_sample",
    )(page_table.reshape(-1), q16, qlat16, lat_new, kr_new, cache_lat, cache_krt)
    return out[:nseq]


def _post_body(alpha, ff_chunks, sample, *refs):
    if sample:
        (x_ref, convout_ref, olat_ref, ga_ref, shf_ref, scf_ref, gf_ref, woc_ref, woa_ref,
         ln1g_ref, ln1b_ref, wg_ref, wu_ref, wd_ref, ln2g_ref, ln2b_ref, wuv_ref, y_ref) = refs
    else:
        (x_ref, convout_ref, attn_ref, ga_ref, shf_ref, scf_ref, gf_ref, woc_ref, woa_ref,
         ln1g_ref, ln1b_ref, wg_ref, wu_ref, wd_ref, ln2g_ref, ln2b_ref, y_ref) = refs
    tm = x_ref.shape[1]

    def mod(ref, rows):
        return ref[0] if ref.shape[1] == 1 else ref[0, rows, :]

    sub = min(tm, POST_ROWS)
    chunks = [pl.ds(r0, sub) for r0 in range(0, tm, sub)]

    def out_proj(rows):
        if sample:
            attn = _dot(olat_ref[0, rows, :].astype(BF16), wuv_ref[...]).astype(BF16)
        else:
            attn = attn_ref[0, rows, :]
        return _dot(convout_ref[0, rows, :], woc_ref[...]) + _dot(attn, woa_ref[...])

    def norm1(rows, a):
        return _layer_norm(alpha * x_ref[0, rows, :] + (1.0 + mod(ga_ref, rows)) * a, ln1g_ref[...], ln1b_ref[...])

    def ffn(rows, x1):
        ub = (x1 * (1.0 + mod(scf_ref, rows)) + mod(shf_ref, rows)).astype(BF16)
        gu = [(_dot(ub, wg_ref[:, lo:hi]), _dot(ub, wu_ref[:, lo:hi])) for lo, hi in ff_chunks]
        f = None
        for (lo, hi), (g, up) in zip(ff_chunks, gu):
            hmid = (g * jax.nn.sigmoid(g) * up).astype(BF16)
            part = _dot(hmid, wd_ref[lo:hi, :])
            f = part if f is None else f + part
        return f

    def norm2(rows, x1, f):
        y_ref[0, rows, :] = _layer_norm(alpha * x1 + (1.0 + mod(gf_ref, rows)) * f, ln2g_ref[...], ln2b_ref[...])

    a = [out_proj(r) for r in chunks]
    x1 = [norm1(r, ai) for r, ai in zip(chunks, a)]
    f = [None] * len(chunks)
    for i, r in enumerate(chunks):
        f[i] = ffn(r, x1[i])
        if i > 0:
            norm2(chunks[i - 1], x1[i - 1], f[i - 1])
    norm2(chunks[-1], x1[-1], f[-1])


def _post(alpha, sample, tm, x, convout, attn, mod, weights, extra):
    nb, s, d = x.shape
    d_ff = weights[5].shape[1]
    cut = min(d_ff, pl.cdiv(d_ff // 2, MXU_TILE) * MXU_TILE)
    ff_chunks = tuple(c for c in ((0, cut), (cut, d_ff)) if c[1] > c[0])
    row = lambda b, i: (b, i, 0)
    in_specs = [pl.BlockSpec((1, tm, d), row),
                pl.BlockSpec((1, tm, convout.shape[2]), row),
                pl.BlockSpec((1, tm, attn.shape[2]), row)]
    in_specs += [_mod_spec(mod, d, col) for col in (2, 3, 4, 5)]
    in_specs += [_const_spec(w.shape) for w in weights]
    in_specs += [_const_spec(e.shape) for e in extra]
    return pl.pallas_call(
        functools.partial(_post_body, alpha, ff_chunks, sample),
        grid=(nb, s // tm), in_specs=in_specs,
        out_specs=pl.BlockSpec((1, tm, d), row),
        out_shape=jax.ShapeDtypeStruct((nb, s, d), F32),
        compiler_params=pltpu.CompilerParams(dimension_semantics=("arbitrary", "arbitrary"),
                                             vmem_limit_bytes=VMEM_LIMIT),
        name="post_sample" if sample else "post_prompt",
    )(x, convout, attn, mod, mod, mod, mod, *weights, *extra)


def _rope_tables(pos, rope, nope):
    half = rope // 2
    inv = 1.0 / (ROPE_THETA ** (jnp.arange(0, rope, 2, dtype=F32) / rope))
    ang = pos.astype(F32)[:, None] * inv[None, :]
    cos, sin = jnp.cos(ang), jnp.sin(ang)
    n = pos.shape[0]
    z = lambda w: jnp.zeros((n, w), F32)
    pad = LANES - nope - rope
    ta = jnp.concatenate([jnp.ones((n, nope), F32), cos, cos, z(pad)], axis=1)
    tb = jnp.concatenate([z(nope + half), sin, z(pad)], axis=1)
    tc = jnp.concatenate([z(nope), -sin, z(half + pad)], axis=1)
    return ta, tb, tc


def kernel(x_prompt, x_sample, cache_latent, cache_k_rope, state_conv, page_table, c_prompt, c_sample,
           w_ada, b_ada, w_in, conv_w, g_q, g_kv, w_uq, w_uk, w_uv, w_o, ln1_g, ln1_b, w_gate, w_up,
           w_down, ln2_g, ln2_b):
    depth = w_ada.shape[0]
    nb, seq, d = x_prompt.shape
    ns, dec_seq, _ = x_sample.shape
    assert dec_seq == 1
    conv_dim = conv_w.shape[2]
    q_rank = g_q.shape[1]
    kv_rank = g_kv.shape[1]
    n_heads, nope = w_uk.shape[2], w_uk.shape[3]
    rope = w_uq.shape[3] - nope
    v_dim = w_uv.shape[3]
    half = rope // 2
    page = cache_latent.shape[2]
    past_len = page_table.shape[1] * page
    alpha = (2 * depth) ** 0.25
    scale = (nope + rope) ** -0.5 * LOG2E
    dims = (conv_dim, q_rank, kv_rank, n_heads, nope, half, scale)
    pad = HEAD_PAD - nope - rope
    assert pad >= 0 and 2 * v_dim == HEAD_PAD

    tabs_p = _rope_tables(jnp.arange(seq), rope, nope)
    tabs_s = _rope_tables(jnp.full((ns,), past_len), rope, nope)

    xp = x_prompt
    xs = x_sample.reshape(1, ns, d)
    c_all = jnp.concatenate([c_prompt, c_sample], axis=0)
    c_all = jnp.pad(c_all, ((0, -(nb + ns) % (2 * SUBLANES)), (0, 0)))
    outs = [[] for _ in range(6)]
    for l in range(depth):
        w_in_l = w_in[l]
        c4 = 3 * conv_dim + q_rank
        c5 = c4 + kv_rank
        wkr = jnp.pad(w_in_l[:, c5:], ((0, 0), (nope, pad)))
        wmain = jnp.concatenate([w_in_l[:, :c4], wkr, w_in_l[:, c4:c5]], axis=1).astype(BF16)
        wq = jnp.pad(w_uq[l], ((0, 0), (0, 0), (0, pad))).reshape(q_rank, n_heads * HEAD_PAD).astype(BF16)
        wk = jnp.pad(w_uk[l], ((0, 0), (0, 0), (0, HEAD_PAD - nope))).reshape(kv_rank, n_heads * HEAD_PAD)
        wk = wk.astype(BF16)
        wv = w_uv[l].reshape(kv_rank, n_heads * v_dim).astype(BF16)
        wukt = jnp.transpose(w_uk[l], (1, 2, 0)).astype(BF16)
        eye = jnp.eye(n_heads, dtype=F32)
        wuv_bd = (w_uv[l].transpose(1, 0, 2)[:, :, None, :] * eye[:, None, :, None])
        wuv_bd = wuv_bd.reshape(n_heads * kv_rank, n_heads * v_dim).astype(BF16)
        woc = w_o[l, :conv_dim].astype(BF16)
        woa = w_o[l, conv_dim:].astype(BF16)
        proj_w = (wmain, conv_w[l], g_q[l].reshape(1, -1), g_kv[l].reshape(1, -1), wq)
        post_w = (woc, woa, ln1_g[l].reshape(1, -1), ln1_b[l].reshape(1, -1), w_gate[l].astype(BF16),
                  w_up[l].astype(BF16), w_down[l].astype(BF16), ln2_g[l].reshape(1, -1),
                  ln2_b[l].reshape(1, -1))

        mod = _ada(c_all, w_ada[l], b_ada[l])
        mod_p = mod[:nb].reshape(nb, 1, 6 * d)
        mod_s = mod[nb:nb + ns].reshape(1, ns, 6 * d)

        convout, q, k, v, lat_p, kr_p, tail = _proj(dims, False, 1024, xp, mod_p, tabs_p, proj_w, (wk, wv))
        attn = _attn_prompt(q, k, v, n_heads)
        xp = _post(alpha, False, 1024, xp, convout, attn, mod_p, post_w, ())

        st = state_conv[l]
        convout_s, q_s, qlat_s, lat_s, kr_s, cin_s = _proj(
            dims, True, ns, xs, mod_s, tabs_s, proj_w, (st[:, 0], st[:, 1], wukt))
        rows_pad = ((0, 0), (0, 2 * SUBLANES - n_heads), (0, 0))
        q16 = jnp.pad(q_s.reshape(ns, n_heads, HEAD_PAD), rows_pad)
        qlat16 = jnp.pad(qlat_s.reshape(ns, n_heads, kv_rank), rows_pad)
        cache_krt = jnp.swapaxes(cache_k_rope[l], 1, 2)
        olat = _attn_sample(page_table, q16, qlat16, lat_s[0], kr_s[0], cache_latent[l], cache_krt,
                            n_heads, nope)
        xs = _post(alpha, True, ns, xs, convout_s, olat.reshape(1, ns, n_heads * kv_rank),
                   mod_s, post_w, (wuv_bd,))

        outs[0].append(lat_p)
        outs[1].append(kr_p)
        outs[2].append(tail[:, SUBLANES - 2:, :])
        outs[3].append(lat_s.reshape(ns, 1, kv_rank))
        outs[4].append(kr_s.reshape(ns, 1, rope))
        outs[5].append(jnp.stack([st[:, 1], cin_s[0]], axis=1))
    return (xp, xs.reshape(ns, 1, d), jnp.stack(outs[0]), jnp.stack(outs[1]), jnp.stack(outs[2]),
            jnp.stack(outs[3]), jnp.stack(outs[4]), jnp.stack(outs[5]))
```

```python
import functools
import math

import jax
import jax.numpy as jnp
from jax import lax
from jax.experimental import pallas as pl
from jax.experimental.pallas import tpu as pltpu

F32 = jnp.float32
BF16 = jnp.bfloat16

ROPE_THETA = 10000.0
LN_EPS = 1e-5
RMS_EPS = 1e-6
NEG_INF = -1e30
LOG2E = math.log2(math.e)

LANES = 128
SUBLANES = 8
MXU_TILE = 256
PROJ_ROWS = 256
POST_ROWS = 256
SKEW = 2
HEAD_PAD = 128
VMEM_LIMIT = 56 * 1024 * 1024


def _dot(a, b):
    return jnp.dot(a, b, preferred_element_type=F32)


def _dot_nt(a, b):
    return lax.dot_general(a, b, (((1,), (1,)), ((), ())), preferred_element_type=F32)


def _layer_norm(r, g, b):
    mu = jnp.mean(r, axis=-1, keepdims=True)
    d = r - mu
    var = jnp.mean(d * d, axis=-1, keepdims=True)
    return d * lax.rsqrt(var + LN_EPS) * g + b


def _rms_norm(x, g):
    return x * lax.rsqrt(jnp.mean(x * x, axis=-1, keepdims=True) + RMS_EPS) * g


def _rope_group(x, ta, tb, tc, half):
    return x * ta + pltpu.roll(x, half, 1) * tb + pltpu.roll(x, LANES - half, 1) * tc


def _ada_body(c_ref, w_ref, b_ref, o_ref):
    c = c_ref[...]
    s = c * jax.nn.sigmoid(c)
    o_ref[...] = _dot(s.astype(BF16), w_ref[...].astype(BF16)) + b_ref[...]


def _ada(c_all, w_ada, b_ada, tn=1024):
    m, d = c_all.shape
    n = w_ada.shape[1]
    return pl.pallas_call(
        _ada_body,
        grid=(n // tn,),
        in_specs=[pl.BlockSpec((m, d), lambda j: (0, 0)),
                  pl.BlockSpec((d, tn), lambda j: (0, j)),
                  pl.BlockSpec((1, tn), lambda j: (0, j))],
        out_specs=pl.BlockSpec((m, tn), lambda j: (0, j)),
        out_shape=jax.ShapeDtypeStruct((m, n), F32),
        compiler_params=pltpu.CompilerParams(dimension_semantics=("arbitrary",),
                                             vmem_limit_bytes=VMEM_LIMIT),
        name="ada",
    )(c_all, w_ada, b_ada.reshape(1, n))


def _proj_body(dims, sample, *refs):
    conv_dim, q_rank, kv_rank, n_heads, nope, half, scale = dims
    if sample:
        (x_ref, sh_ref, sc_ref, ta_ref, tb_ref, tc_ref, wmain_ref, convw_ref, gq_ref, gkv_ref,
         wq_ref, s0_ref, s1_ref, wukt_ref,
         convout_ref, q_ref, qlat_ref, lat_ref, krot_ref, cin_ref) = refs
    else:
        (x_ref, sh_ref, sc_ref, ta_ref, tb_ref, tc_ref, wmain_ref, convw_ref, gq_ref, gkv_ref,
         wq_ref, wk_ref, wv_ref,
         convout_ref, q_ref, k_ref, v_ref, lat_ref, krot_ref, tail_ref, cin_buf) = refs
    tm = x_ref.shape[1]
    c1, c2, c3 = conv_dim, 2 * conv_dim, 3 * conv_dim
    c4 = c3 + q_rank + LANES
    c5 = c4 + kv_rank
    w0 = convw_ref[0:1, :]
    w1 = convw_ref[1:2, :]
    w2 = convw_ref[2:3, :]
    if not sample:
        @pl.when(pl.program_id(1) == 0)
        def _():
            cin_buf[0:SUBLANES, :] = jnp.zeros((SUBLANES, conv_dim), F32)

    def in_proj(rows):
        x = x_ref[0, rows, :]
        sc = sc_ref[0] if sc_ref.shape[1] == 1 else sc_ref[0, rows, :]
        sh = sh_ref[0] if sh_ref.shape[1] == 1 else sh_ref[0, rows, :]
        ub = (x * (1.0 + sc) + sh).astype(BF16)
        return (_dot(ub, wmain_ref[:, 0:c1]), _dot(ub, wmain_ref[:, c1:c2]), _dot(ub, wmain_ref[:, c2:c3]),
                _dot(ub, wmain_ref[:, c3:c4]), _dot(ub, wmain_ref[:, c4:c5]))

    def rest(rows, sub, products):
        h, gb, gc, cq_kr, ckv = products
        cq = cq_kr[:, 0:q_rank]
        kr = cq_kr[:, q_rank:q_rank + LANES]

        conv_in = gc * h
        if sample:
            y = w0 * s0_ref[rows, :] + w1 * s1_ref[rows, :] + w2 * conv_in
            cin_ref[0, rows, :] = conv_in
        else:
            cin_buf[SUBLANES:SUBLANES + sub, :] = conv_in
            y = (w0 * cin_buf[SUBLANES - 2:SUBLANES - 2 + sub, :]
                 + w1 * cin_buf[SUBLANES - 1:SUBLANES - 1 + sub, :] + w2 * conv_in)
            tail = conv_in[sub - SUBLANES:sub, :]
            cin_buf[0:SUBLANES, :] = tail
            tail_ref[0] = tail
        convout_ref[0, rows, :] = (gb * y).astype(BF16)

        ta = ta_ref[rows, :]
        tb = tb_ref[rows, :]
        tc = tc_ref[rows, :]
        cqn = _rms_norm(cq, gq_ref[...]).astype(BF16)
        qf = _dot(cqn, wq_ref[...])
        q_heads = [_rope_group(qf[:, HEAD_PAD * i:HEAD_PAD * (i + 1)], ta, tb, tc, half) * scale
                   for i in range(n_heads)]
        q_ref[0, rows, :] = jnp.concatenate(q_heads, axis=1).astype(q_ref.dtype)

        latent = _rms_norm(ckv, gkv_ref[...])
        lat_ref[0, rows, :] = latent
        krot = _rope_group(kr, ta, tb, tc, half)
        krot_ref[0, rows, :] = krot[:, nope:nope + 2 * half]

        if sample:
            for i in range(n_heads):
                qn = q_heads[i][:, 0:nope].astype(BF16)
                qlat_ref[0, rows, kv_rank * i:kv_rank * (i + 1)] = _dot(qn, wukt_ref[i])
        else:
            latb = latent.astype(BF16)
            kf = _dot(latb, wk_ref[...])
            k_ref[0, rows, :] = (kf + jnp.concatenate([krot] * n_heads, axis=1)).astype(BF16)
            v_ref[0, rows, :] = _dot(latb, wv_ref[...]).astype(BF16)

    sub = min(tm, PROJ_ROWS)
    chunks = [pl.ds(r0, sub) for r0 in range(0, tm, sub)]
    products = [in_proj(rows) for rows in chunks]
    for rows, prod in zip(chunks, products):
        rest(rows, sub, prod)


def _const_spec(shape):
    nd = len(shape)
    return pl.BlockSpec(shape, lambda *_: (0,) * nd, pipeline_mode=pl.Buffered(1))


def _mod_spec(mod, d, col):
    return pl.BlockSpec((1, mod.shape[1], d), lambda b, i: (b, 0, col))


def _proj(dims, sample, tm, x, mod, tabs, weights, extra):
    conv_dim, q_rank, kv_rank, n_heads, nope, half, _ = dims
    nb, s, d = x.shape
    grid = (nb, s // tm)
    row = lambda b, i: (b, i, 0)
    tab_spec = pl.BlockSpec((tm, LANES), lambda b, i: (i, 0))
    wmain, convw, gq, gkv, wq = weights
    in_specs = [pl.BlockSpec((1, tm, d), row), _mod_spec(mod, d, 0), _mod_spec(mod, d, 1),
                tab_spec, tab_spec, tab_spec,
                _const_spec(wmain.shape), _const_spec(convw.shape),
                _const_spec(gq.shape), _const_spec(gkv.shape), _const_spec(wq.shape)]
    in_specs += [_const_spec(e.shape) for e in extra]
    hq = n_heads * HEAD_PAD
    out_shape = [jax.ShapeDtypeStruct((nb, s, conv_dim), BF16)]
    out_specs = [pl.BlockSpec((1, tm, conv_dim), row)]
    if sample:
        out_shape += [jax.ShapeDtypeStruct((nb, s, hq), F32),
                      jax.ShapeDtypeStruct((nb, s, n_heads * kv_rank), F32)]
        out_specs += [pl.BlockSpec((1, tm, hq), row), pl.BlockSpec((1, tm, n_heads * kv_rank), row)]
    else:
        out_shape += [jax.ShapeDtypeStruct((nb, s, hq), BF16), jax.ShapeDtypeStruct((nb, s, hq), BF16),
                      jax.ShapeDtypeStruct((nb, s, n_heads * (hq // n_heads // 2)), BF16)]
        out_specs += [pl.BlockSpec((1, tm, hq), row), pl.BlockSpec((1, tm, hq), row),
                      pl.BlockSpec((1, tm, n_heads * (hq // n_heads // 2)), row)]
    out_shape += [jax.ShapeDtypeStruct((nb, s, kv_rank), F32), jax.ShapeDtypeStruct((nb, s, 2 * half), F32)]
    out_specs += [pl.BlockSpec((1, tm, kv_rank), row), pl.BlockSpec((1, tm, 2 * half), row)]
    scratch = []
    if sample:
        out_shape += [jax.ShapeDtypeStruct((nb, s, conv_dim), F32)]
        out_specs += [pl.BlockSpec((1, tm, conv_dim), row)]
    else:
        out_shape += [jax.ShapeDtypeStruct((nb, SUBLANES, conv_dim), F32)]
        out_specs += [pl.BlockSpec((1, SUBLANES, conv_dim), lambda b, i: (b, 0, 0))]
        scratch = [pltpu.VMEM((SUBLANES + tm, conv_dim), F32)]
    return pl.pallas_call(
        functools.partial(_proj_body, dims, sample),
        grid=grid, in_specs=in_specs, out_specs=out_specs, out_shape=out_shape,
        scratch_shapes=scratch,
        compiler_params=pltpu.CompilerParams(dimension_semantics=("arbitrary", "arbitrary"),
                                             vmem_limit_bytes=VMEM_LIMIT),
        name="proj_sample" if sample else "proj_prompt",
    )(x, mod, mod, *tabs, *weights, *extra)


def _attn_prompt_body(v_dim, q_ref, k_ref, v_ref, o_ref, vt_ref):
    tq = q_ref.shape[1]
    tk = tq
    qi = pl.program_id(2)
    n_h = q_ref.shape[2] // HEAD_PAD
    vrows = vt_ref.shape[1]

    @pl.when(qi == 0)
    def _():
        vt = v_ref[0].astype(F32).T.astype(BF16)
        tail = (lax.broadcasted_iota(jnp.int32, (vrows - v_dim, vt.shape[1]), 0) == 0).astype(BF16)
        for hh in range(n_h):
            vt_ref[hh, 0:v_dim, :] = vt[v_dim * hh:v_dim * (hh + 1), :]
            vt_ref[hh, v_dim:vrows, :] = tail

    keys = lax.broadcasted_iota(jnp.int32, (tk, tq), 0)
    queries = lax.broadcasted_iota(jnp.int32, (tk, tq), 1)
    qts = [q_ref[0, :, HEAD_PAD * hh:HEAD_PAD * (hh + 1)].astype(F32).T.astype(BF16) for hh in range(n_h)]

    def step(j, carry, masked):
        at = pl.ds(pl.multiple_of(j * tk, tk), tk)

        def scores(hh):
            s = _dot(k_ref[0, at, HEAD_PAD * hh:HEAD_PAD * (hh + 1)], qts[hh])
            return jnp.where(keys <= queries, s, NEG_INF) if masked else s

        new = []
        ahead = [scores(hh) for hh in range(min(SKEW, n_h))]
        for hh, (m, acc) in enumerate(carry):
            s = ahead.pop(0)
            if hh + SKEW < n_h:
                ahead.append(scores(hh + SKEW))
            m_new = jnp.maximum(m, jnp.max(s, axis=0, keepdims=True))
            p = jnp.exp2(s - m_new).astype(BF16)
            acc = jnp.exp2(m - m_new) * acc + _dot(vt_ref[hh, :, at], p)
            new.append((m_new, acc))
        return tuple(new)

    init = tuple((jnp.full((1, tq), NEG_INF, F32), jnp.zeros((vrows, tq), F32)) for _ in range(n_h))
    carry = lax.fori_loop(0, qi, functools.partial(step, masked=False), init)
    carry = step(qi, carry, True)
    ot = jnp.concatenate([acc[0:v_dim] / acc[v_dim:v_dim + 1] for _, acc in carry], axis=0)
    o_ref[0] = ot.T.astype(o_ref.dtype)


def _attn_prompt(q, k, v, n_heads, tq=512, hg=8):
    b, s, _ = q.shape
    v_dim = v.shape[2] // n_heads
    assert n_heads % hg == 0 and (hg * v_dim) % LANES == 0
    return pl.pallas_call(
        functools.partial(_attn_prompt_body, v_dim),
        grid=(b, n_heads // hg, s // tq),
        in_specs=[pl.BlockSpec((1, tq, hg * HEAD_PAD), lambda bi, hp, qi: (bi, qi, hp)),
                  pl.BlockSpec((1, s, hg * HEAD_PAD), lambda bi, hp, qi: (bi, 0, hp)),
                  pl.BlockSpec((1, s, hg * v_dim), lambda bi, hp, qi: (bi, 0, hp))],
        out_specs=pl.BlockSpec((1, tq, hg * v_dim), lambda bi, hp, qi: (bi, qi, hp)),
        out_shape=jax.ShapeDtypeStruct((b, s, n_heads * v_dim), BF16),
        scratch_shapes=[pltpu.VMEM((hg, v_dim + 2 * SUBLANES, s), BF16)],
        compiler_params=pltpu.CompilerParams(dimension_semantics=("arbitrary",) * 3,
                                             vmem_limit_bytes=VMEM_LIMIT),
        name="attn_prompt",
    )(q, k, v)


def _attn_sample_body(geom, pt_ref, q_ref, qlat_ref, latn_ref, krn_ref, lat_hbm, krt_hbm, o_ref,
                      lat_buf, kr_buf, kb_buf, s_buf, st_ref, acc_ref, p_ref, sems):
    nseq, n_pages, ch, page, nope, rope = geom
    n_chunks = n_pages // ch
    n_steps = nseq * n_chunks
    n_heads = o_ref.shape[1]
    hp = q_ref.shape[1]
    n_slots = lat_buf.shape[0]
    n_kb = kb_buf.shape[0]

    def page_copies(pg, slot, j):
        cols = pl.ds(j * page, page)
        return (pltpu.make_async_copy(lat_hbm.at[pg], lat_buf.at[slot, cols], sems.at[0, slot]),
                pltpu.make_async_copy(krt_hbm.at[pg], kr_buf.at[slot, j], sems.at[1, slot]))

    def start_chunk(g, slot):
        for j in range(ch):
            for cp in page_copies(pt_ref[g * ch + j], slot, j):
                cp.start()

    def wait_chunk(slot):
        pltpu.make_async_copy(lat_buf.at[slot], lat_buf.at[slot], sems.at[0, slot]).wait()
        pltpu.make_async_copy(kr_buf.at[slot], kr_buf.at[slot], sems.at[1, slot]).wait()

    def put(i, col):
        st_ref[i] = jnp.broadcast_to(col, st_ref.shape[1:])

    def substep(a, u):
        m, l, alpha = st_ref[0][:, 0:1], st_ref[1][:, 0:1], st_ref[2][:, 0:1]
        acc, p = acc_ref[...], p_ref[...]
        wait_chunk(u)

        acc = alpha * acc + _dot(p, kb_buf[(u - 2) % n_kb])
        out_row = jnp.minimum(lax.div(jnp.maximum(a - 2, 0), n_chunks), nseq)
        o_ref[out_row] = (acc / l)[0:n_heads, :]

        seq_a = jnp.minimum(lax.div(a, n_chunks), nseq - 1)
        refill = jnp.minimum(a + n_slots - 1, n_steps - 1) * ch
        rslot = (u + n_slots - 1) % n_slots
        pages = []
        for j in range(ch):
            rows = pl.ds(j * page, page)
            kbj = lat_buf[u, rows, :].astype(BF16)
            kb_buf[u % n_kb, rows, :] = kbj
            pages.append(kbj)
            for cp in page_copies(pt_ref[refill + j], rslot, j):
                cp.start()
        kb = jnp.concatenate(pages, axis=0)
        krb = jnp.concatenate([kr_buf[u, j] for j in range(ch)], axis=1).astype(BF16)
        qa = q_ref[seq_a]
        s_buf[u % 2] = (_dot_nt(qlat_ref[seq_a].astype(BF16), kb)
                        + _dot(qa[:, nope:nope + rope].astype(BF16), krb))

        b = jnp.maximum(a - 1, 0)
        seq_b = jnp.minimum(lax.div(b, n_chunks), nseq - 1)
        first = lax.rem(b, n_chunks) == 0
        qb = q_ref[seq_b]
        qlat = qlat_ref[seq_b]
        latn = latn_ref[pl.ds(seq_b, 1), :]
        krn = krn_ref[pl.ds(seq_b, 1), :]
        s_new = (jnp.sum(qlat * latn, axis=-1, keepdims=True)
                 + jnp.sum(qb[:, nope:nope + rope] * krn, axis=-1, keepdims=True))
        m = jnp.where(first, s_new, m)
        l = jnp.where(first, 1.0, l)
        acc = jnp.where(first, jnp.broadcast_to(latn, acc.shape), acc)
        s = s_buf[(u - 1) % 2]
        m_new = jnp.maximum(m, jnp.max(s, axis=-1, keepdims=True))
        p = jnp.exp2(s - m_new)
        alpha = jnp.exp2(m - m_new)
        put(0, m_new)
        put(1, alpha * l + jnp.sum(p, axis=-1, keepdims=True))
        put(2, alpha)
        acc_ref[...] = acc
        p_ref[...] = p.astype(BF16)

    i = pl.program_id(0)

    @pl.when(i == 0)
    def _():
        for c in range(n_slots - 1):
            start_chunk(i + c, c)
        for k in range(2, n_kb):
            kb_buf[k] = jnp.zeros(kb_buf.shape[1:], BF16)
        s_buf[1] = jnp.zeros(s_buf.shape[1:], F32)
        p_ref[...] = jnp.zeros(p_ref.shape, BF16)
        acc_ref[...] = jnp.zeros(acc_ref.shape, F32)
        put(0, jnp.zeros((hp, 1), F32))
        put(1, jnp.ones((hp, 1), F32))
        put(2, jnp.zeros((hp, 1), F32))

    for u in range(n_slots):
        pl.when(i >= 0)(functools.partial(substep, n_slots * i + u, u))

    @pl.when(i == pl.num_programs(0) - 1)
    def _():
        for slot in range(n_slots - 1):
            wait_chunk(slot)


def _attn_sample(page_table, q16, qlat16, lat_new, kr_new, cache_lat, cache_krt, n_heads, nope,
                 ch=32, n_slots=4):
    nseq, n_pages = page_table.shape
    _, page, kv_rank = cache_lat.shape
    rope = cache_krt.shape[1]
    assert n_pages % ch == 0 and n_slots % 4 == 0
    chk = ch * page
    hp = q16.shape[1]
    geom = (nseq, n_pages, ch, page, nope, rope)
    positions = nseq * (n_pages // ch) + 2
    whole = lambda a: pl.BlockSpec(a.shape, lambda i, pt: (0,) * a.ndim)
    grid_spec = pltpu.PrefetchScalarGridSpec(
        num_scalar_prefetch=1,
        grid=(pl.cdiv(positions, n_slots),),
        in_specs=[whole(q16), whole(qlat16), whole(lat_new), whole(kr_new),
                  pl.BlockSpec(memory_space=pl.ANY), pl.BlockSpec(memory_space=pl.ANY)],
        out_specs=pl.BlockSpec((nseq + 1, n_heads, kv_rank), lambda i, pt: (0, 0, 0)),
        scratch_shapes=[pltpu.VMEM((n_slots, chk, kv_rank), F32),
                        pltpu.VMEM((n_slots, ch, rope, page), F32),
                        pltpu.VMEM((4, chk, kv_rank), BF16),
                        pltpu.VMEM((2, hp, chk), F32),
                        pltpu.VMEM((3, hp, LANES), F32),
                        pltpu.VMEM((hp, kv_rank), F32),
                        pltpu.VMEM((hp, chk), BF16),
                        pltpu.SemaphoreType.DMA((2, n_slots))],
    )
    out = pl.pallas_call(
        functools.partial(_attn_sample_body, geom),
        grid_spec=grid_spec,
        out_shape=jax.ShapeDtypeStruct((nseq + 1, n_heads, kv_rank), F32),
        compiler_params=pltpu.CompilerParams(dimension_semantics=("arbitrary",),
                                             vmem_limit_bytes=VMEM_LIMIT),
        name="attn_sample",
    )(page_table.reshape(-1), q16, qlat16, lat_new, kr_new, cache_lat, cache_krt)
    return out[:nseq]


def _post_body(alpha, ff_chunks, sample, *refs):
    if sample:
        (x_ref, convout_ref, olat_ref, ga_ref, shf_ref, scf_ref, gf_ref, woc_ref, woa_ref,
         ln1g_ref, ln1b_ref, wg_ref, wu_ref, wd_ref, ln2g_ref, ln2b_ref, wuv_ref, y_ref) = refs
    else:
        (x_ref, convout_ref, attn_ref, ga_ref, shf_ref, scf_ref, gf_ref, woc_ref, woa_ref,
         ln1g_ref, ln1b_ref, wg_ref, wu_ref, wd_ref, ln2g_ref, ln2b_ref, y_ref) = refs
    tm = x_ref.shape[1]

    def mod(ref, rows):
        return ref[0] if ref.shape[1] == 1 else ref[0, rows, :]

    sub = min(tm, POST_ROWS)
    chunks = [pl.ds(r0, sub) for r0 in range(0, tm, sub)]

    def out_proj(rows):
        if sample:
            attn = _dot(olat_ref[0, rows, :].astype(BF16), wuv_ref[...]).astype(BF16)
        else:
            attn = attn_ref[0, rows, :]
        return _dot(convout_ref[0, rows, :], woc_ref[...]) + _dot(attn, woa_ref[...])

    def norm1(rows, a):
        return _layer_norm(alpha * x_ref[0, rows, :] + (1.0 + mod(ga_ref, rows)) * a, ln1g_ref[...], ln1b_ref[...])

    def ffn(rows, x1):
        ub = (x1 * (1.0 + mod(scf_ref, rows)) + mod(shf_ref, rows)).astype(BF16)
        gu = [(_dot(ub, wg_ref[:, lo:hi]), _dot(ub, wu_ref[:, lo:hi])) for lo, hi in ff_chunks]
        f = None
        for (lo, hi), (g, up) in zip(ff_chunks, gu):
            hmid = (g * jax.nn.sigmoid(g) * up).astype(BF16)
            part = _dot(hmid, wd_ref[lo:hi, :])
            f = part if f is None else f + part
        return f

    def norm2(rows, x1, f):
        y_ref[0, rows, :] = _layer_norm(alpha * x1 + (1.0 + mod(gf_ref, rows)) * f, ln2g_ref[...], ln2b_ref[...])

    a = [out_proj(r) for r in chunks]
    x1 = [norm1(r, ai) for r, ai in zip(chunks, a)]
    f = [None] * len(chunks)
    for i, r in enumerate(chunks):
        f[i] = ffn(r, x1[i])
        if i > 0:
            norm2(chunks[i - 1], x1[i - 1], f[i - 1])
    norm2(chunks[-1], x1[-1], f[-1])


def _post(alpha, sample, tm, x, convout, attn, mod, weights, extra):
    nb, s, d = x.shape
    d_ff = weights[5].shape[1]
    cut = min(d_ff, pl.cdiv(d_ff // 2, MXU_TILE) * MXU_TILE)
    ff_chunks = tuple(c for c in ((0, cut), (cut, d_ff)) if c[1] > c[0])
    row = lambda b, i: (b, i, 0)
    in_specs = [pl.BlockSpec((1, tm, d), row),
                pl.BlockSpec((1, tm, convout.shape[2]), row),
                pl.BlockSpec((1, tm, attn.shape[2]), row)]
    in_specs += [_mod_spec(mod, d, col) for col in (2, 3, 4, 5)]
    in_specs += [_const_spec(w.shape) for w in weights]
    in_specs += [_const_spec(e.shape) for e in extra]
    return pl.pallas_call(
        functools.partial(_post_body, alpha, ff_chunks, sample),
        grid=(nb, s // tm), in_specs=in_specs,
        out_specs=pl.BlockSpec((1, tm, d), row),
        out_shape=jax.ShapeDtypeStruct((nb, s, d), F32),
        compiler_params=pltpu.CompilerParams(dimension_semantics=("arbitrary", "arbitrary"),
                                             vmem_limit_bytes=VMEM_LIMIT),
        name="post_sample" if sample else "post_prompt",
    )(x, convout, attn, mod, mod, mod, mod, *weights, *extra)


def _rope_tables(pos, rope, nope):
    half = rope // 2
    inv = 1.0 / (ROPE_THETA ** (jnp.arange(0, rope, 2, dtype=F32) / rope))
    ang = pos.astype(F32)[:, None] * inv[None, :]
    cos, sin = jnp.cos(ang), jnp.sin(ang)
    n = pos.shape[0]
    z = lambda w: jnp.zeros((n, w), F32)
    pad = LANES - nope - rope
    ta = jnp.concatenate([jnp.ones((n, nope), F32), cos, cos, z(pad)], axis=1)
    tb = jnp.concatenate([z(nope + half), sin, z(pad)], axis=1)
    tc = jnp.concatenate([z(nope), -sin, z(half + pad)], axis=1)
    return ta, tb, tc


def kernel(x_prompt, x_sample, cache_latent, cache_k_rope, state_conv, page_table, c_prompt, c_sample,
           w_ada, b_ada, w_in, conv_w, g_q, g_kv, w_uq, w_uk, w_uv, w_o, ln1_g, ln1_b, w_gate, w_up,
           w_down, ln2_g, ln2_b):
    depth = w_ada.shape[0]
    nb, seq, d = x_prompt.shape
    ns, dec_seq, _ = x_sample.shape
    assert dec_seq == 1
    conv_dim = conv_w.shape[2]
    q_rank = g_q.shape[1]
    kv_rank = g_kv.shape[1]
    n_heads, nope = w_uk.shape[2], w_uk.shape[3]
    rope = w_uq.shape[3] - nope
    v_dim = w_uv.shape[3]
    half = rope // 2
    page = cache_latent.shape[2]
    past_len = page_table.shape[1] * page
    alpha = (2 * depth) ** 0.25
    scale = (nope + rope) ** -0.5 * LOG2E
    dims = (conv_dim, q_rank, kv_rank, n_heads, nope, half, scale)
    pad = HEAD_PAD - nope - rope
    assert pad >= 0 and 2 * v_dim == HEAD_PAD

    tabs_p = _rope_tables(jnp.arange(seq), rope, nope)
    tabs_s = _rope_tables(jnp.full((ns,), past_len), rope, nope)

    xp = x_prompt
    xs = x_sample.reshape(1, ns, d)
    c_all = jnp.concatenate([c_prompt, c_sample], axis=0)
    c_all = jnp.pad(c_all, ((0, -(nb + ns) % (2 * SUBLANES)), (0, 0)))
    outs = [[] for _ in range(6)]
    for l in range(depth):
        w_in_l = w_in[l]
        c4 = 3 * conv_dim + q_rank
        c5 = c4 + kv_rank
        wkr = jnp.pad(w_in_l[:, c5:], ((0, 0), (nope, pad)))
        wmain = jnp.concatenate([w_in_l[:, :c4], wkr, w_in_l[:, c4:c5]], axis=1).astype(BF16)
        wq = jnp.pad(w_uq[l], ((0, 0), (0, 0), (0, pad))).reshape(q_rank, n_heads * HEAD_PAD).astype(BF16)
        wk = jnp.pad(w_uk[l], ((0, 0), (0, 0), (0, HEAD_PAD - nope))).reshape(kv_rank, n_heads * HEAD_PAD)
        wk = wk.astype(BF16)
        wv = w_uv[l].reshape(kv_rank, n_heads * v_dim).astype(BF16)
        wukt = jnp.transpose(w_uk[l], (1, 2, 0)).astype(BF16)
        eye = jnp.eye(n_heads, dtype=F32)
        wuv_bd = (w_uv[l].transpose(1, 0, 2)[:, :, None, :] * eye[:, None, :, None])
        wuv_bd = wuv_bd.reshape(n_heads * kv_rank, n_heads * v_dim).astype(BF16)
        woc = w_o[l, :conv_dim].astype(BF16)
        woa = w_o[l, conv_dim:].astype(BF16)
        proj_w = (wmain, conv_w[l], g_q[l].reshape(1, -1), g_kv[l].reshape(1, -1), wq)
        post_w = (woc, woa, ln1_g[l].reshape(1, -1), ln1_b[l].reshape(1, -1), w_gate[l].astype(BF16),
                  w_up[l].astype(BF16), w_down[l].astype(BF16), ln2_g[l].reshape(1, -1),
                  ln2_b[l].reshape(1, -1))

        mod = _ada(c_all, w_ada[l], b_ada[l])
        mod_p = mod[:nb].reshape(nb, 1, 6 * d)
        mod_s = mod[nb:nb + ns].reshape(1, ns, 6 * d)

        convout, q, k, v, lat_p, kr_p, tail = _proj(dims, False, 1024, xp, mod_p, tabs_p, proj_w, (wk, wv))
        attn = _attn_prompt(q, k, v, n_heads)
        xp = _post(alpha, False, 1024, xp, convout, attn, mod_p, post_w, ())

        st = state_conv[l]
        convout_s, q_s, qlat_s, lat_s, kr_s, cin_s = _proj(
            dims, True, ns, xs, mod_s, tabs_s, proj_w, (st[:, 0], st[:, 1], wukt))
        rows_pad = ((0, 0), (0, 2 * SUBLANES - n_heads), (0, 0))
        q16 = jnp.pad(q_s.reshape(ns, n_heads, HEAD_PAD), rows_pad)
        qlat16 = jnp.pad(qlat_s.reshape(ns, n_heads, kv_rank), rows_pad)
        cache_krt = jnp.swapaxes(cache_k_rope[l], 1, 2)
        olat = _attn_sample(page_table, q16, qlat16, lat_s[0], kr_s[0], cache_latent[l], cache_krt,
                            n_heads, nope)
        xs = _post(alpha, True, ns, xs, convout_s, olat.reshape(1, ns, n_heads * kv_rank),
                   mod_s, post_w, (wuv_bd,))

        outs[0].append(lat_p)
        outs[1].append(kr_p)
        outs[2].append(tail[:, SUBLANES - 2:, :])
        outs[3].append(lat_s.reshape(ns, 1, kv_rank))
        outs[4].append(kr_s.reshape(ns, 1, rope))
        outs[5].append(jnp.stack([st[:, 1], cin_s[0]], axis=1))
    return (xp, xs.reshape(ns, 1, d), jnp.stack(outs[0]), jnp.stack(outs[1]), jnp.stack(outs[2]),
            jnp.stack(outs[3]), jnp.stack(outs[4]), jnp.stack(outs[5]))
```
